```python
import jax, jax.numpy as jnp
from jax import lax
import numpy as np

D_MODEL = 1024
BATCH = 8
SEQ = 4096
DEPTH = 4

CHUNK = 128
GM_GROUPS = 8
GM_WIDTH = 512
GM_GROUP_DIM = GM_WIDTH // GM_GROUPS
N_HEADS = 8
N_KV_HEADS = 2
HEAD_DIM = 64
Q_PER_KV = N_HEADS // N_KV_HEADS
Q_WIDTH = N_HEADS * HEAD_DIM
KV_WIDTH = N_KV_HEADS * HEAD_DIM
WINDOW = 128
BLOCK = 128
N_BRANCH = 2
D_FF = ((-(-8 * D_MODEL // 3) + 255) // 256) * 256
IN_WIDTH = 2 * GM_WIDTH + Q_WIDTH + 2 * KV_WIDTH + N_BRANCH * D_MODEL
N_MOD = 6
EPS = 1e-6
NEG_INF = -1e30

kernel_name = "hybrid_gmlp_swa_alibi_adaln_encoder"


def rmsnorm(x, gain):
    xf = x.astype(jnp.float32)
    y = xf * lax.rsqrt(jnp.mean(xf * xf, axis=-1, keepdims=True) + EPS)
    return (y * gain.astype(jnp.float32)).astype(x.dtype)


def alibi_slopes():
    h = jnp.arange(1, N_HEADS + 1, dtype=jnp.float32)
    return jnp.exp2(-8.0 * h / N_HEADS)


def gmlp_branch(u, v, v_gain, w_s, b_s):
    b, s, _ = u.shape
    u = jax.nn.gelu(u)
    v = rmsnorm(jax.nn.gelu(v), v_gain)
    v = v.reshape(b, s // CHUNK, CHUNK, GM_GROUPS, GM_GROUP_DIM)
    mixed = jnp.einsum('gts,bnsgc->bntgc', w_s, v) + b_s.T[None, None, :, :, None]
    return u * mixed.reshape(b, s, GM_WIDTH)


def window_attention_branch(q, k, v, q_gain, k_gain, sink):
    b, s, _ = q.shape
    nb = s // BLOCK
    q = rmsnorm(q.reshape(b, s, N_HEADS, HEAD_DIM), q_gain)
    k = rmsnorm(k.reshape(b, s, N_KV_HEADS, HEAD_DIM), k_gain)
    v = v.reshape(b, s, N_KV_HEADS, HEAD_DIM)
    qb = q.reshape(b, nb, BLOCK, N_KV_HEADS, Q_PER_KV, HEAD_DIM)

    def band(t):
        tp = jnp.pad(t, ((0, 0), (BLOCK, BLOCK), (0, 0), (0, 0)))
        tp = tp.reshape(b, nb + 2, BLOCK, N_KV_HEADS, HEAD_DIM)
        return jnp.concatenate([tp[:, :-2], tp[:, 1:-1], tp[:, 2:]], axis=2)

    kb, vb = band(k), band(v)
    logits = jnp.einsum('bnqkgd,bnskd->bnkgqs', qb, kb).astype(jnp.float32) * (HEAD_DIM ** -0.5)

    qi = jnp.arange(BLOCK)[:, None]
    kj = jnp.arange(3 * BLOCK)[None, :]
    dist = jnp.abs(kj - BLOCK - qi)
    kpos = jnp.arange(nb)[:, None] * BLOCK - BLOCK + jnp.arange(3 * BLOCK)[None, :]
    valid = (dist <= WINDOW)[None] & ((kpos >= 0) & (kpos < s))[:, None, :]
    slopes = alibi_slopes().reshape(N_KV_HEADS, Q_PER_KV)
    bias = -slopes[:, :, None, None] * dist.astype(jnp.float32)[None, None]
    logits = jnp.where(valid[None, :, None, None], logits + bias[None, None], NEG_INF)

    sink_logit = jnp.broadcast_to(
        sink.astype(jnp.float32).reshape(1, 1, N_KV_HEADS, Q_PER_KV, 1, 1),
        logits.shape[:-1] + (1,))
    probs = jax.nn.softmax(jnp.concatenate([logits, sink_logit], axis=-1), axis=-1)[..., :-1]
    out = jnp.einsum('bnkgqs,bnskd->bnqkgd', probs.astype(vb.dtype), vb)
    return out.reshape(b, s, Q_WIDTH)


def hybrid_layer(x, c, w_ada, b_ada, norm1_g, w_in, gm_v_g, gm_w_s, gm_b_s,
                 q_norm_g, k_norm_g, attn_sink, w_a, w_b, w_o, norm2_g,
                 w_ffn_in, w_ffn_out):
    mod = jax.nn.silu(c) @ w_ada + b_ada
    sh1, sc1, gt1, sh2, sc2, gt2 = jnp.split(mod[:, None, :], N_MOD, axis=-1)

    h = rmsnorm(x, norm1_g) * (1.0 + sc1) + sh1
    z = h @ w_in
    cuts = np.cumsum([GM_WIDTH, GM_WIDTH, Q_WIDTH, KV_WIDTH, KV_WIDTH, D_MODEL]).tolist()
    u, v, q, k, va, gate_a, gate_b = jnp.split(z, cuts, axis=-1)
    a_out = gmlp_branch(u, v, gm_v_g, gm_w_s, gm_b_s) @ w_a
    b_out = window_attention_branch(q, k, va, q_norm_g, k_norm_g, attn_sink) @ w_b
    merged = jax.nn.sigmoid(gate_a) * a_out + jax.nn.sigmoid(gate_b) * b_out
    x = x + gt1 * (merged @ w_o)

    h2 = rmsnorm(x, norm2_g) * (1.0 + sc2) + sh2
    f_gate, f_up = jnp.split(h2 @ w_ffn_in, 2, axis=-1)
    x = x + gt2 * ((jax.nn.silu(f_gate) * f_up) @ w_ffn_out)
    return x


def setup_inputs(seed: int = 0) -> dict:
    key = jax.random.key(seed)
    ks = jax.random.split(key, 20)
    f32 = jnp.float32

    def nrm(k, shape, scale):
        return jax.random.normal(k, shape, f32) * scale

    def gain(k, shape):
        return 1.0 + 0.02 * jax.random.normal(k, shape, f32)

    L, D = DEPTH, D_MODEL
    return {
        "x": nrm(ks[0], (BATCH, SEQ, D), 1.0),
        "c": nrm(ks[1], (BATCH, D), 1.0),
        "w_ada": nrm(ks[2], (L, D, N_MOD * D), 0.5 * D ** -0.5),
        "b_ada": nrm(ks[3], (L, N_MOD * D), 0.02),
        "norm1_g": gain(ks[4], (L, D)),
        "w_in": nrm(ks[5], (L, D, IN_WIDTH), D ** -0.5),
        "gm_v_g": gain(ks[6], (L, GM_WIDTH)),
        "gm_w_s": nrm(ks[7], (L, GM_GROUPS, CHUNK, CHUNK), CHUNK ** -0.5),
        "gm_b_s": gain(ks[8], (L, GM_GROUPS, CHUNK)),
        "q_norm_g": gain(ks[9], (L, HEAD_DIM)),
        "k_norm_g": gain(ks[10], (L, HEAD_DIM)),
        "attn_sink": nrm(ks[11], (L, N_HEADS), 0.5),
        "w_a": nrm(ks[12], (L, GM_WIDTH, D), GM_WIDTH ** -0.5),
        "w_b": nrm(ks[13], (L, Q_WIDTH, D), Q_WIDTH ** -0.5),
        "w_o": nrm(ks[14], (L, D, D), D ** -0.5),
        "norm2_g": gain(ks[15], (L, D)),
        "w_ffn_in": nrm(ks[16], (L, D, 2 * D_FF), D ** -0.5),
        "w_ffn_out": nrm(ks[17], (L, D_FF, D), D_FF ** -0.5),
    }


def reference(x, c, w_ada, b_ada, norm1_g, w_in, gm_v_g, gm_w_s, gm_b_s,
              q_norm_g, k_norm_g, attn_sink, w_a, w_b, w_o, norm2_g,
              w_ffn_in, w_ffn_out):
    for l in range(DEPTH):
        x = hybrid_layer(x, c, w_ada[l], b_ada[l], norm1_g[l], w_in[l], gm_v_g[l],
                         gm_w_s[l], gm_b_s[l], q_norm_g[l], k_norm_g[l], attn_sink[l],
                         w_a[l], w_b[l], w_o[l], norm2_g[l], w_ffn_in[l], w_ffn_out[l])
    return x
```

```python
import functools

import numpy as np
import jax
import jax.numpy as jnp
from jax import lax
from jax.experimental import pallas as pl
from jax.experimental.pallas import tpu as pltpu

D_MODEL = 1024
CHUNK = 128
GM_GROUPS = 8
GM_WIDTH = 512
GM_GROUP_DIM = GM_WIDTH // GM_GROUPS
N_HEADS = 8
N_KV_HEADS = 2
HEAD_DIM = 64
Q_PER_KV = N_HEADS // N_KV_HEADS
Q_WIDTH = N_HEADS * HEAD_DIM
KV_WIDTH = N_KV_HEADS * HEAD_DIM
BLOCK = 128
D_FF = ((-(-8 * D_MODEL // 3) + 255) // 256) * 256
N_MOD = 6
EPS = 1e-6
NEG_INF = -1e30

COL_U = 0
COL_Q = 2 * GM_WIDTH
COL_K = COL_Q + Q_WIDTH
COL_V = COL_K + KV_WIDTH
COL_GA = COL_V + KV_WIDTH
COL_GB = COL_GA + D_MODEL
IN_WIDTH = COL_GB + D_MODEL

MIX_TILE = 512
FFN_TILE = 512
FFN_CHUNK = 256
ADA_TILE = 1536
VMEM_LIMIT_BYTES = 56 * 1024 * 1024

BF16 = jnp.bfloat16
F32 = jnp.float32


def _rms(x, gain):
    ms = jnp.mean(x * x, axis=-1, keepdims=True)
    return x * lax.rsqrt(ms + EPS) * gain


def _dot(a, b):
    return jnp.dot(a, b, preferred_element_type=F32)


def _const_spec(shape):
    zeros = (0,) * len(shape)
    return pl.BlockSpec(shape, lambda *_: zeros, pipeline_mode=pl.Buffered(1))


def _ada_kernel(c_ref, w_ref, b_ref, o_ref):
    c = c_ref[...]
    a = (c * jax.nn.sigmoid(c)).astype(BF16)
    o_ref[0] = _dot(a, w_ref[0].astype(BF16)) + b_ref[0]


def _ada_mod(c, w_ada, b_ada):
    n_layers, d, width = w_ada.shape
    batch = c.shape[0]
    return pl.pallas_call(
        _ada_kernel,
        grid=(n_layers, width // ADA_TILE),
        in_specs=[
            pl.BlockSpec((batch, d), lambda l, j: (0, 0)),
            pl.BlockSpec((1, d, ADA_TILE), lambda l, j: (l, 0, j)),
            pl.BlockSpec((1, 1, ADA_TILE), lambda l, j: (l, 0, j)),
        ],
        out_specs=pl.BlockSpec((1, batch, ADA_TILE), lambda l, j: (l, 0, j)),
        out_shape=jax.ShapeDtypeStruct((n_layers, batch, width), F32),
        compiler_params=pltpu.CompilerParams(
            dimension_semantics=("arbitrary", "arbitrary"), vmem_limit_bytes=VMEM_LIMIT_BYTES),
        name="adaln_mod",
    )(c, w_ada, b_ada.reshape(n_layers, 1, width))


def _attn_bias_tables():
    qi = np.arange(BLOCK)[:, None]
    kj = np.arange(3 * BLOCK)[None, :]
    dist = np.abs(kj - BLOCK - qi)
    slopes = np.exp2(-8.0 * np.arange(1, N_HEADS + 1) / N_HEADS).astype(np.float32)
    bias = -slopes[:, None, None] * dist[None].astype(np.float32)
    tables = []
    for lo, hi in ((BLOCK, 3 * BLOCK), (0, 3 * BLOCK), (0, 2 * BLOCK)):
        valid = (dist <= BLOCK) & (kj >= lo) & (kj < hi)
        tables.append(np.where(valid[None], bias, np.float32(NEG_INF)))
    t = np.stack(tables).astype(np.float32)
    return t.reshape(3, N_KV_HEADS, Q_PER_KV * BLOCK, 3 * BLOCK)


def _mixer_kernel(x_ref, xp_ref, xn_ref, mod_ref, g1_ref, win_ref, vg_ref, ws_ref, bs_ref,
                  qg_ref, kg_ref, sink_ref, bias_ref, wa_ref, wb_ref, wo_ref, o_ref,
                  q_buf, k_ext, v_ext, a_buf, att_buf, *, tile, n_seq_blocks):
    n_blk = tile // BLOCK
    i = pl.program_id(1)
    sh1 = mod_ref[0, 0:1, :]
    sc1 = mod_ref[0, 1:2, :]
    gt1 = mod_ref[0, 2:3, :]
    g1 = g1_ref[...]

    def modnorm(xv):
        return (_rms(xv, g1) * (1.0 + sc1) + sh1).astype(BF16)

    x = x_ref[0]
    hb = modnorm(x)

    w_kv = win_ref[:, COL_K:COL_GA]
    kg = kg_ref[...]
    for row0, h_rows in ((0, modnorm(xp_ref[0])), (BLOCK, hb), (BLOCK + tile, modnorm(xn_ref[0]))):
        kv = _dot(h_rows, w_kv)
        rows = kv.shape[0]
        for j in range(N_KV_HEADS):
            k_j = kv[:, j * HEAD_DIM:(j + 1) * HEAD_DIM]
            k_ext[j, row0:row0 + rows, :] = _rms(k_j, kg).astype(BF16)
            v_j = kv[:, KV_WIDTH + j * HEAD_DIM:KV_WIDTH + (j + 1) * HEAD_DIM]
            v_ext[j, row0:row0 + rows, :] = v_j.astype(BF16)

    zq = _dot(hb, win_ref[:, COL_Q:COL_K])
    qg = qg_ref[...] * (HEAD_DIM ** -0.5)
    for h in range(N_HEADS):
        q_h = _rms(zq[:, h * HEAD_DIM:(h + 1) * HEAD_DIM], qg).astype(BF16)
        j, g = divmod(h, Q_PER_KV)
        for n in range(n_blk):
            q_buf[j, n, g * BLOCK:(g + 1) * BLOCK, :] = q_h[n * BLOCK:(n + 1) * BLOCK]

    def attn_block(n, carry):
        r0 = pl.multiple_of(n * BLOCK, BLOCK)
        gb = i * n_blk + n
        tix = jnp.where(gb == 0, 0, jnp.where(gb == n_seq_blocks - 1, 2, 1))
        for j in range(N_KV_HEADS):
            q = q_buf[j, n]
            k = k_ext[j, pl.ds(r0, 3 * BLOCK), :]
            v = v_ext[j, pl.ds(r0, 3 * BLOCK), :]
            s = lax.dot_general(q, k, (((1,), (1,)), ((), ())), preferred_element_type=F32)
            s = s + bias_ref[tix, j]
            es, dens = [], []
            for g in range(Q_PER_KV):
                sg = s[g * BLOCK:(g + 1) * BLOCK]
                sink = sink_ref[j * Q_PER_KV + g]
                m = jnp.maximum(jnp.max(sg, axis=-1, keepdims=True), sink)
                e = jnp.exp(sg - m)
                dens.append(jnp.sum(e, axis=-1, keepdims=True) + jnp.exp(sink - m))
                es.append(e.astype(BF16))
            o = _dot(jnp.concatenate(es, axis=0), v)
            for g in range(Q_PER_KV):
                h = j * Q_PER_KV + g
                att_buf[pl.ds(r0, BLOCK), h * HEAD_DIM:(h + 1) * HEAD_DIM] = (
                    o[g * BLOCK:(g + 1) * BLOCK] / dens[g]).astype(BF16)
        return carry

    lax.fori_loop(0, n_blk, attn_block, 0)

    uv_ref_cols = win_ref[:, COL_U:COL_Q]
    uv = _dot(hb, uv_ref_cols)
    vg = vg_ref[...]
    for c in range(n_blk):
        rows = slice(c * CHUNK, (c + 1) * CHUNK)
        u = jax.nn.gelu(uv[rows, :GM_WIDTH])
        v = _rms(jax.nn.gelu(uv[rows, GM_WIDTH:]), vg).astype(BF16)
        mixed = jnp.concatenate(
            [_dot(ws_ref[g], v[:, g * GM_GROUP_DIM:(g + 1) * GM_GROUP_DIM]) for g in range(GM_GROUPS)],
            axis=1)
        a_buf[rows, :] = (u * (mixed + bs_ref[...])).astype(BF16)

    gate_a = jax.nn.sigmoid(_dot(hb, win_ref[:, COL_GA:COL_GB]))
    merged = gate_a * _dot(a_buf[...], wa_ref[...])
    gate_b = jax.nn.sigmoid(_dot(hb, win_ref[:, COL_GB:IN_WIDTH]))
    merged = merged + gate_b * _dot(att_buf[...], wb_ref[...])
    o_ref[0] = x + gt1 * _dot(merged.astype(BF16), wo_ref[...])


def _mixer(x, mod, g1, w_in, vg, ws, bs_full, qg, kg, sink, bias_tbl, w_a, w_b, w_o):
    batch, seq, d = x.shape
    tile = MIX_TILE
    n_blk = tile // BLOCK
    n_seq_blocks = seq // BLOCK
    kern = functools.partial(_mixer_kernel, tile=tile, n_seq_blocks=n_seq_blocks)
    return pl.pallas_call(
        kern,
        grid=(batch, seq // tile),
        in_specs=[
            pl.BlockSpec((1, tile, d), lambda b, i: (b, i, 0)),
            pl.BlockSpec((1, BLOCK, d), lambda b, i: (b, jnp.maximum(i * n_blk - 1, 0), 0)),
            pl.BlockSpec((1, BLOCK, d), lambda b, i: (b, jnp.minimum((i + 1) * n_blk, n_seq_blocks - 1), 0)),
            pl.BlockSpec((1, N_MOD, d), lambda b, i: (b, 0, 0)),
            _const_spec(g1.shape),
            _const_spec(w_in.shape),
            _const_spec(vg.shape),
            _const_spec(ws.shape),
            _const_spec(bs_full.shape),
            _const_spec(qg.shape),
            _const_spec(kg.shape),
            pl.BlockSpec(memory_space=pltpu.SMEM),
            _const_spec(bias_tbl.shape),
            _const_spec(w_a.shape),
            _const_spec(w_b.shape),
            _const_spec(w_o.shape),
        ],
        out_specs=pl.BlockSpec((1, tile, d), lambda b, i: (b, i, 0)),
        out_shape=jax.ShapeDtypeStruct(x.shape, x.dtype),
        scratch_shapes=[
            pltpu.VMEM((N_KV_HEADS, n_blk, Q_PER_KV * BLOCK, HEAD_DIM), BF16),
            pltpu.VMEM((N_KV_HEADS, tile + 2 * BLOCK, HEAD_DIM), BF16),
            pltpu.VMEM((N_KV_HEADS, tile + 2 * BLOCK, HEAD_DIM), BF16),
            pltpu.VMEM((tile, GM_WIDTH), BF16),
            pltpu.VMEM((tile, Q_WIDTH), BF16),
        ],
        compiler_params=pltpu.CompilerParams(
            dimension_semantics=("arbitrary", "arbitrary"), vmem_limit_bytes=VMEM_LIMIT_BYTES),
        name="token_mixer",
    )(x, x, x, mod, g1, w_in, vg, ws, bs_full, qg, kg, sink, bias_tbl, w_a, w_b, w_o)


def _ffn_kernel(x_ref, mod_ref, g2_ref, w1_ref, w2_ref, o_ref):
    sh2 = mod_ref[0, 3:4, :]
    sc2 = mod_ref[0, 4:5, :]
    gt2 = mod_ref[0, 5:6, :]
    x = x_ref[0]
    hb = (_rms(x, g2_ref[...]) * (1.0 + sc2) + sh2).astype(BF16)
    acc = jnp.zeros(x.shape, F32)
    for s in range(0, D_FF, FFN_CHUNK):
        f_gate = _dot(hb, w1_ref[:, s:s + FFN_CHUNK])
        f_up = _dot(hb, w1_ref[:, D_FF + s:D_FF + s + FFN_CHUNK])
        act = (f_gate * jax.nn.sigmoid(f_gate) * f_up).astype(BF16)
        acc = acc + _dot(act, w2_ref[s:s + FFN_CHUNK, :])
    o_ref[0] = x + gt2 * acc


def _ffn(x, mod, g2, w1, w2):
    batch, seq, d = x.shape
    tile = FFN_TILE
    return pl.pallas_call(
        _ffn_kernel,
        grid=(batch, seq // tile),
        in_specs=[
            pl.BlockSpec((1, tile, d), lambda b, i: (b, i, 0)),
            pl.BlockSpec((1, N_MOD, d), lambda b, i: (b, 0, 0)),
            _const_spec(g2.shape),
            _const_spec(w1.shape),
            _const_spec(w2.shape),
        ],
        out_specs=pl.BlockSpec((1, tile, d), lambda b, i: (b, i, 0)),
        out_shape=jax.ShapeDtypeStruct(x.shape, x.dtype),
        compiler_params=pltpu.CompilerParams(
            dimension_semantics=("arbitrary", "arbitrary"), vmem_limit_bytes=VMEM_LIMIT_BYTES),
        name="swiglu_ffn",
    )(x, mod, g2, w1, w2)


def kernel(x, c, w_ada, b_ada, norm1_g, w_in, gm_v_g, gm_w_s, gm_b_s, q_norm_g, k_norm_g, attn_sink,
           w_a, w_b, w_o, norm2_g, w_ffn_in, w_ffn_out):
    n_layers = w_ada.shape[0]
    batch = x.shape[0]
    assert x.shape[1] % MIX_TILE == 0 and x.shape[1] % FFN_TILE == 0 and D_FF % FFN_CHUNK == 0

    mod = _ada_mod(c, w_ada, b_ada).reshape(n_layers, batch, N_MOD, D_MODEL)
    bias_tbl = jnp.asarray(_attn_bias_tables())
    bs_full = jnp.repeat(jnp.swapaxes(gm_b_s, 1, 2), GM_GROUP_DIM, axis=2)

    for l in range(n_layers):
        x = _mixer(
            x, mod[l], norm1_g[l][None], w_in[l].astype(BF16), gm_v_g[l][None],
            gm_w_s[l].astype(BF16), bs_full[l], q_norm_g[l][None], k_norm_g[l][None], attn_sink[l],
            bias_tbl, w_a[l].astype(BF16), w_b[l].astype(BF16), w_o[l].astype(BF16))
        x = _ffn(x, mod[l], norm2_g[l][None], w_ffn_in[l].astype(BF16), w_ffn_out[l].astype(BF16))
    return x
```

```python
import functools

import numpy as np
import jax
import jax.numpy as jnp
from jax import lax
from jax.experimental import pallas as pl
from jax.experimental.pallas import tpu as pltpu

D_MODEL = 1024
CHUNK = 128
GM_GROUPS = 8
GM_WIDTH = 512
GM_GROUP_DIM = GM_WIDTH // GM_GROUPS
N_HEADS = 8
N_KV_HEADS = 2
HEAD_DIM = 64
Q_PER_KV = N_HEADS // N_KV_HEADS
Q_WIDTH = N_HEADS * HEAD_DIM
KV_WIDTH = N_KV_HEADS * HEAD_DIM
BLOCK = 128
D_FF = ((-(-8 * D_MODEL // 3) + 255) // 256) * 256
N_MOD = 6
EPS = 1e-6
NEG_INF = -1e30

COL_U = 0
COL_Q = 2 * GM_WIDTH
COL_K = COL_Q + Q_WIDTH
COL_V = COL_K + KV_WIDTH
COL_GA = COL_V + KV_WIDTH
COL_GB = COL_GA + D_MODEL
IN_WIDTH = COL_GB + D_MODEL

PAIR = 2 * HEAD_DIM
SEG_WIDTH = 256
MIX_TILE = 512
FFN_TILE = 512
FFN_CHUNK = 256
ADA_TILE = 1536
VMEM_LIMIT_BYTES = 56 * 1024 * 1024

BF16 = jnp.bfloat16
F32 = jnp.float32


def _rms(x, gain):
    ms = jnp.mean(x * x, axis=-1, keepdims=True)
    return x * lax.rsqrt(ms + EPS) * gain


def _dot(a, b):
    return jnp.dot(a, b, preferred_element_type=F32)


def _const_spec(shape):
    zeros = (0,) * len(shape)
    return pl.BlockSpec(shape, lambda *_: zeros, pipeline_mode=pl.Buffered(1))


def _ada_kernel(c_ref, w_ref, b_ref, o_ref):
    c = c_ref[...]
    a = (c * jax.nn.sigmoid(c)).astype(BF16)
    o_ref[0] = _dot(a, w_ref[0].astype(BF16)) + b_ref[0]


def _ada_mod(c, w_ada, b_ada):
    n_layers, d, width = w_ada.shape
    batch = c.shape[0]
    return pl.pallas_call(
        _ada_kernel,
        grid=(n_layers, width // ADA_TILE),
        in_specs=[
            pl.BlockSpec((batch, d), lambda l, j: (0, 0)),
            pl.BlockSpec((1, d, ADA_TILE), lambda l, j: (l, 0, j)),
            pl.BlockSpec((1, 1, ADA_TILE), lambda l, j: (l, 0, j)),
        ],
        out_specs=pl.BlockSpec((1, batch, ADA_TILE), lambda l, j: (l, 0, j)),
        out_shape=jax.ShapeDtypeStruct((n_layers, batch, width), F32),
        compiler_params=pltpu.CompilerParams(
            dimension_semantics=("arbitrary", "arbitrary"), vmem_limit_bytes=VMEM_LIMIT_BYTES),
        name="adaln_mod",
    )(c, w_ada, b_ada.reshape(n_layers, 1, width))


def _attn_bias_tables():
    qi = np.arange(BLOCK)[:, None]
    kj = np.arange(3 * BLOCK)[None, :]
    dist = np.abs(kj - BLOCK - qi)
    slopes = np.exp2(-8.0 * np.arange(1, N_HEADS + 1) / N_HEADS).astype(np.float32)
    bias = -slopes[:, None, None] * dist[None].astype(np.float32)
    tables = []
    for lo, hi in ((BLOCK, 3 * BLOCK), (0, 3 * BLOCK), (0, 2 * BLOCK)):
        valid = (dist <= BLOCK) & (kj >= lo) & (kj < hi)
        tables.append(np.where(valid[None], bias, np.float32(NEG_INF)))
    t = np.stack(tables).astype(np.float32)
    t = t.reshape(3, N_KV_HEADS, 2, 2, BLOCK, 3 * BLOCK)
    return t.transpose(0, 1, 2, 4, 3, 5).reshape(3, N_KV_HEADS, 2 * BLOCK, 6 * BLOCK)


def _segment_ones(width):
    idx = np.arange(width) // HEAD_DIM
    return (idx[:, None] == idx[None, :]).astype(np.float32)


def _head_sumsq(x, seg):
    sq = x * x
    hi = sq.astype(BF16)
    lo = (sq - hi.astype(F32)).astype(BF16)
    return _dot(hi, seg) + _dot(lo, seg)


def _mixer_kernel(x_ref, xp_ref, xn_ref, mod_ref, g1_ref, win_ref, vg_ref, ws_ref, bs_ref,
                  qg_ref, kg_ref, sink_ref, bias_ref, seg_ref, wa_ref, wb_ref, wo_ref, o_ref,
                  q_buf, k_ext, v_ext, e_buf, a_buf, att_buf, *, tile, n_seq_blocks):
    n_blk = tile // BLOCK
    i = pl.program_id(1)
    sh1 = mod_ref[0, 0:1, :]
    gt1 = mod_ref[0, 2:3, :]
    g1s = g1_ref[...] * (1.0 + mod_ref[0, 1:2, :])

    def modnorm(xv):
        ms = jnp.mean(xv * xv, axis=-1, keepdims=True)
        return (xv * lax.rsqrt(ms + EPS) * g1s + sh1).astype(BF16)

    x = x_ref[0]
    hb = modnorm(x)

    w_kv = win_ref[:, COL_K:COL_GA]
    kg = kg_ref[...]
    seg_k = seg_ref[0:KV_WIDTH, 0:KV_WIDTH]
    for row0, h_rows in ((0, modnorm(xp_ref[0])), (BLOCK, hb), (BLOCK + tile, modnorm(xn_ref[0]))):
        kv = _dot(h_rows, w_kv)
        rows = kv.shape[0]
        k = kv[:, :KV_WIDTH]
        k = k * lax.rsqrt(_head_sumsq(k, seg_k) * (1.0 / HEAD_DIM) + EPS) * kg
        v = kv[:, KV_WIDTH:]
        low = lax.broadcasted_iota(jnp.int32, k.shape, 1) < HEAD_DIM
        for src, dst in ((k, k_ext), (v, v_ext)):
            swapped = pltpu.roll(src, HEAD_DIM, axis=1)
            for j in range(N_KV_HEADS):
                own, other = (src, swapped) if j == 0 else (swapped, src)
                dst[j, 0, row0:row0 + rows, :] = jnp.where(low, own, 0.0).astype(BF16)
                dst[j, 1, row0:row0 + rows, :] = jnp.where(low, 0.0, other).astype(BF16)

    zq = _dot(hb, win_ref[:, COL_Q:COL_K])
    qg = qg_ref[...] * (HEAD_DIM ** -0.5)
    seg = seg_ref[...]
    pairs_per_seg = SEG_WIDTH // PAIR
    for half in range(Q_WIDTH // SEG_WIDTH):
        cols = slice(half * SEG_WIDTH, (half + 1) * SEG_WIDTH)
        zh = zq[:, cols]
        qn = (zh * lax.rsqrt(_head_sumsq(zh, seg) * (1.0 / HEAD_DIM) + EPS) * qg[:, cols]).astype(BF16)
        for pp in range(pairs_per_seg):
            j, p = divmod(half * pairs_per_seg + pp, 2)
            for n in range(n_blk):
                q_buf[j, n, p * BLOCK:(p + 1) * BLOCK, :] = (
                    qn[n * BLOCK:(n + 1) * BLOCK, pp * PAIR:(pp + 1) * PAIR])

    low_lane = lax.broadcasted_iota(jnp.int32, (BLOCK, PAIR), 1) < HEAD_DIM

    def attn_block(n, carry):
        r0 = pl.multiple_of(n * BLOCK, BLOCK)
        band = pl.ds(r0, 3 * BLOCK)
        gb = i * n_blk + n
        tix = jnp.where(gb == 0, 0, jnp.where(gb == n_seq_blocks - 1, 2, 1))
        for j in range(N_KV_HEADS):
            q = q_buf[j, n]
            k = jnp.concatenate([k_ext[j, 0, band, :], k_ext[j, 1, band, :]], axis=0)
            s = lax.dot_general(q, k, (((1,), (1,)), ((), ())), preferred_element_type=F32)
            s = s + bias_ref[tix, j]
            dens = []
            for p in range(2):
                for par in range(2):
                    sg = s[p * BLOCK:(p + 1) * BLOCK, par * 3 * BLOCK:(par + 1) * 3 * BLOCK]
                    sink = sink_ref[j * Q_PER_KV + 2 * p + par]
                    m = jnp.maximum(jnp.max(sg, axis=-1, keepdims=True), sink)
                    e = jnp.exp(sg - m)
                    dens.append(jnp.sum(e, axis=-1, keepdims=True) + jnp.exp(sink - m))
                    e_buf[p * BLOCK:(p + 1) * BLOCK, par * 3 * BLOCK:(par + 1) * 3 * BLOCK] = e.astype(BF16)
            v = jnp.concatenate([v_ext[j, 0, band, :], v_ext[j, 1, band, :]], axis=0)
            o = _dot(e_buf[...], v)
            for p in range(2):
                den = jnp.where(low_lane, dens[2 * p], dens[2 * p + 1])
                att_buf[pl.ds(r0, BLOCK), (2 * j + p) * PAIR:(2 * j + p + 1) * PAIR] = (
                    o[p * BLOCK:(p + 1) * BLOCK] / den).astype(BF16)
        return carry

    lax.fori_loop(0, n_blk, attn_block, 0)

    uv = _dot(hb, win_ref[:, COL_U:COL_Q])
    vg = vg_ref[...]
    for c in range(n_blk):
        rows = slice(c * CHUNK, (c + 1) * CHUNK)
        u = jax.nn.gelu(uv[rows, :GM_WIDTH])
        v = _rms(jax.nn.gelu(uv[rows, GM_WIDTH:]), vg).astype(BF16)
        mixed = jnp.concatenate(
            [_dot(ws_ref[g], v[:, g * GM_GROUP_DIM:(g + 1) * GM_GROUP_DIM]) for g in range(GM_GROUPS)],
            axis=1)
        a_buf[rows, :] = (u * (mixed + bs_ref[...])).astype(BF16)

    gate_a = jax.nn.sigmoid(_dot(hb, win_ref[:, COL_GA:COL_GB]))
    merged = gate_a * _dot(a_buf[...], wa_ref[...])
    gate_b = jax.nn.sigmoid(_dot(hb, win_ref[:, COL_GB:IN_WIDTH]))
    merged = merged + gate_b * _dot(att_buf[...], wb_ref[...])
    o_ref[0] = x + gt1 * _dot(merged.astype(BF16), wo_ref[...])


def _mixer(x, mod, g1, w_in, vg, ws, bs_full, qg, kg, sink, bias_tbl, seg, w_a, w_b, w_o):
    batch, seq, d = x.shape
    tile = MIX_TILE
    n_blk = tile // BLOCK
    n_seq_blocks = seq // BLOCK
    kern = functools.partial(_mixer_kernel, tile=tile, n_seq_blocks=n_seq_blocks)
    return pl.pallas_call(
        kern,
        grid=(batch, seq // tile),
        in_specs=[
            pl.BlockSpec((1, tile, d), lambda b, i: (b, i, 0)),
            pl.BlockSpec((1, BLOCK, d), lambda b, i: (b, jnp.maximum(i * n_blk - 1, 0), 0)),
            pl.BlockSpec((1, BLOCK, d), lambda b, i: (b, jnp.minimum((i + 1) * n_blk, n_seq_blocks - 1), 0)),
            pl.BlockSpec((1, N_MOD, d), lambda b, i: (b, 0, 0)),
            _const_spec(g1.shape),
            _const_spec(w_in.shape),
            _const_spec(vg.shape),
            _const_spec(ws.shape),
            _const_spec(bs_full.shape),
            _const_spec(qg.shape),
            _const_spec(kg.shape),
            pl.BlockSpec(memory_space=pltpu.SMEM),
            _const_spec(bias_tbl.shape),
            _const_spec(seg.shape),
            _const_spec(w_a.shape),
            _const_spec(w_b.shape),
            _const_spec(w_o.shape),
        ],
        out_specs=pl.BlockSpec((1, tile, d), lambda b, i: (b, i, 0)),
        out_shape=jax.ShapeDtypeStruct(x.shape, x.dtype),
        scratch_shapes=[
            pltpu.VMEM((N_KV_HEADS, n_blk, 2 * BLOCK, PAIR), BF16),
            pltpu.VMEM((N_KV_HEADS, 2, tile + 2 * BLOCK, PAIR), BF16),
            pltpu.VMEM((N_KV_HEADS, 2, tile + 2 * BLOCK, PAIR), BF16),
            pltpu.VMEM((2 * BLOCK, 6 * BLOCK), BF16),
            pltpu.VMEM((tile, GM_WIDTH), BF16),
            pltpu.VMEM((tile, Q_WIDTH), BF16),
        ],
        compiler_params=pltpu.CompilerParams(
            dimension_semantics=("arbitrary", "arbitrary"), vmem_limit_bytes=VMEM_LIMIT_BYTES),
        name="token_mixer",
    )(x, x, x, mod, g1, w_in, vg, ws, bs_full, qg, kg, sink, bias_tbl, seg, w_a, w_b, w_o)


def _ffn_kernel(x_ref, mod_ref, g2_ref, w1_ref, w2_ref, o_ref):
    sh2 = mod_ref[0, 3:4, :]
    sc2 = mod_ref[0, 4:5, :]
    gt2 = mod_ref[0, 5:6, :]
    x = x_ref[0]
    hb = (_rms(x, g2_ref[...]) * (1.0 + sc2) + sh2).astype(BF16)
    acc = jnp.zeros(x.shape, F32)
    for s in range(0, D_FF, FFN_CHUNK):
        f_gate = _dot(hb, w1_ref[:, s:s + FFN_CHUNK])
        f_up = _dot(hb, w1_ref[:, D_FF + s:D_FF + s + FFN_CHUNK])
        act = (f_gate * jax.nn.sigmoid(f_gate) * f_up).astype(BF16)
        acc = acc + _dot(act, w2_ref[s:s + FFN_CHUNK, :])
    o_ref[0] = x + gt2 * acc


def _ffn(x, mod, g2, w1, w2):
    batch, seq, d = x.shape
    tile = FFN_TILE
    return pl.pallas_call(
        _ffn_kernel,
        grid=(batch, seq // tile),
        in_specs=[
            pl.BlockSpec((1, tile, d), lambda b, i: (b, i, 0)),
            pl.BlockSpec((1, N_MOD, d), lambda b, i: (b, 0, 0)),
            _const_spec(g2.shape),
            _const_spec(w1.shape),
            _const_spec(w2.shape),
        ],
        out_specs=pl.BlockSpec((1, tile, d), lambda b, i: (b, i, 0)),
        out_shape=jax.ShapeDtypeStruct(x.shape, x.dtype),
        compiler_params=pltpu.CompilerParams(
            dimension_semantics=("arbitrary", "arbitrary"), vmem_limit_bytes=VMEM_LIMIT_BYTES),
        name="swiglu_ffn",
    )(x, mod, g2, w1, w2)


def kernel(x, c, w_ada, b_ada, norm1_g, w_in, gm_v_g, gm_w_s, gm_b_s, q_norm_g, k_norm_g, attn_sink,
           w_a, w_b, w_o, norm2_g, w_ffn_in, w_ffn_out):
    n_layers = w_ada.shape[0]
    batch = x.shape[0]
    assert x.shape[1] % MIX_TILE == 0 and x.shape[1] % FFN_TILE == 0 and D_FF % FFN_CHUNK == 0

    mod = _ada_mod(c, w_ada, b_ada).reshape(n_layers, batch, N_MOD, D_MODEL)
    bias_tbl = jnp.asarray(_attn_bias_tables())
    seg = jnp.asarray(_segment_ones(SEG_WIDTH), dtype=BF16)
    bs_full = jnp.repeat(jnp.swapaxes(gm_b_s, 1, 2), GM_GROUP_DIM, axis=2)
    qg = jnp.tile(q_norm_g, (1, N_HEADS))[:, None, :]
    kg = jnp.tile(k_norm_g, (1, N_KV_HEADS))[:, None, :]

    for l in range(n_layers):
        x = _mixer(
            x, mod[l], norm1_g[l][None], w_in[l].astype(BF16), gm_v_g[l][None],
            gm_w_s[l].astype(BF16), bs_full[l], qg[l], kg[l], attn_sink[l],
            bias_tbl, seg, w_a[l].astype(BF16), w_b[l].astype(BF16), w_o[l].astype(BF16))
        x = _ffn(x, mod[l], norm2_g[l][None], w_ffn_in[l].astype(BF16), w_ffn_out[l].astype(BF16))
    return x
```

```python
import functools
import math

import numpy as np
import jax
import jax.numpy as jnp
from jax import lax
from jax.experimental import pallas as pl
from jax.experimental.pallas import tpu as pltpu

D_MODEL = 1024
CHUNK = 128
GM_GROUPS = 8
GM_WIDTH = 512
GM_GROUP_DIM = GM_WIDTH // GM_GROUPS
N_HEADS = 8
N_KV_HEADS = 2
HEAD_DIM = 64
Q_PER_KV = N_HEADS // N_KV_HEADS
Q_WIDTH = N_HEADS * HEAD_DIM
KV_WIDTH = N_KV_HEADS * HEAD_DIM
BLOCK = 128
D_FF = ((-(-8 * D_MODEL // 3) + 255) // 256) * 256
N_MOD = 6
EPS = 1e-6
NEG_INF = -1e30
LOG2E = math.log2(math.e)

COL_U = 0
COL_Q = 2 * GM_WIDTH
COL_K = COL_Q + Q_WIDTH
COL_V = COL_K + KV_WIDTH
COL_GA = COL_V + KV_WIDTH
COL_GB = COL_GA + D_MODEL
IN_WIDTH = COL_GB + D_MODEL

PAIR = 2 * HEAD_DIM
SEG_WIDTH = 256
MIX_TILE = 512
FFN_TILE = 512
FFN_CHUNK = 256
ADA_TILE = 1536
VMEM_LIMIT_BYTES = 56 * 1024 * 1024

BF16 = jnp.bfloat16
F32 = jnp.float32


def _rms(x, gain):
    ms = jnp.mean(x * x, axis=-1, keepdims=True)
    return x * lax.rsqrt(ms + EPS) * gain


def _dot(a, b):
    return jnp.dot(a, b, preferred_element_type=F32)


def _const_spec(shape):
    zeros = (0,) * len(shape)
    return pl.BlockSpec(shape, lambda *_: zeros, pipeline_mode=pl.Buffered(1))


def _layer_spec(shape, layer):
    index = (layer,) + (0,) * (len(shape) - 1)
    return pl.BlockSpec((1,) + tuple(shape[1:]), lambda *_: index, pipeline_mode=pl.Buffered(1))


def _ada_kernel(c_ref, w_ref, b_ref, o_ref):
    c = c_ref[...]
    a = (c * jax.nn.sigmoid(c)).astype(BF16)
    o_ref[0] = _dot(a, w_ref[0].astype(BF16)) + b_ref[0]


def _ada_mod(c, w_ada, b_ada):
    n_layers, d, width = w_ada.shape
    batch = c.shape[0]
    return pl.pallas_call(
        _ada_kernel,
        grid=(n_layers, width // ADA_TILE),
        in_specs=[
            pl.BlockSpec((batch, d), lambda l, j: (0, 0)),
            pl.BlockSpec((1, d, ADA_TILE), lambda l, j: (l, 0, j)),
            pl.BlockSpec((1, 1, ADA_TILE), lambda l, j: (l, 0, j)),
        ],
        out_specs=pl.BlockSpec((1, batch, ADA_TILE), lambda l, j: (l, 0, j)),
        out_shape=jax.ShapeDtypeStruct((n_layers, batch, width), F32),
        compiler_params=pltpu.CompilerParams(
            dimension_semantics=("arbitrary", "arbitrary"), vmem_limit_bytes=VMEM_LIMIT_BYTES),
        name="adaln_mod",
    )(c, w_ada, b_ada.reshape(n_layers, 1, width))


def _attn_bias_tables():
    qi = np.arange(BLOCK)[:, None]
    kj = np.arange(3 * BLOCK)[None, :]
    dist = np.abs(kj - BLOCK - qi)
    slopes = np.exp2(-8.0 * np.arange(1, N_HEADS + 1) / N_HEADS).astype(np.float32)
    bias = -slopes[:, None, None] * dist[None].astype(np.float32) * np.float32(LOG2E)
    tables = []
    for lo, hi in ((BLOCK, 3 * BLOCK), (0, 3 * BLOCK), (0, 2 * BLOCK)):
        valid = (dist <= BLOCK) & (kj >= lo) & (kj < hi)
        tables.append(np.where(valid[None], bias, np.float32(NEG_INF)))
    t = np.stack(tables).astype(np.float32)
    t = t.reshape(3, N_KV_HEADS, 2, 2, BLOCK, 3 * BLOCK)
    return t.transpose(0, 1, 2, 4, 3, 5).reshape(3, N_KV_HEADS, 2 * BLOCK, 6 * BLOCK)


def _segment_ones(width):
    idx = np.arange(width) // HEAD_DIM
    return (idx[:, None] == idx[None, :]).astype(np.float32)


def _head_sumsq(x, seg):
    sq = x * x
    hi = sq.astype(BF16)
    lo = (sq - hi.astype(F32)).astype(BF16)
    return _dot(hi, seg) + _dot(lo, seg)


def _mixer_kernel(x_ref, xp_ref, xn_ref, mod_ref, g1_ref, win_ref, vg_ref, ws_ref, bs_ref,
                  qg_ref, kg_ref, sink_ref, bias_ref, seg_ref, wa_ref, wb_ref, wo_ref, o_ref,
                  q_buf, k_ext, v_ext, e_buf, a_buf, att_buf, gate_buf, *, tile, n_seq_blocks, layer):
    n_blk = tile // BLOCK
    n_units = N_KV_HEADS * n_blk
    gate_chunk = 2 * D_MODEL // n_units
    assert gate_chunk * n_blk == D_MODEL
    i = pl.program_id(1)
    sh1 = mod_ref[0, 0, 0:1, :]
    gt1 = mod_ref[0, 0, 2:3, :]
    g1s = g1_ref[0] * (1.0 + mod_ref[0, 0, 1:2, :])

    def modnorm(xv):
        ms = jnp.mean(xv * xv, axis=-1, keepdims=True)
        return (xv * lax.rsqrt(ms + EPS) * g1s + sh1).astype(BF16)

    x = x_ref[0]
    hb = modnorm(x)

    w_kv = win_ref[0, :, COL_K:COL_GA]
    kg = kg_ref[0]
    seg_k = seg_ref[0:KV_WIDTH, 0:KV_WIDTH]
    for row0, h_rows in ((0, modnorm(xp_ref[0])), (BLOCK, hb), (BLOCK + tile, modnorm(xn_ref[0]))):
        kv = _dot(h_rows, w_kv)
        rows = slice(row0, row0 + kv.shape[0])
        k = kv[:, :KV_WIDTH]
        k = k * lax.rsqrt(_head_sumsq(k, seg_k) * (1.0 / HEAD_DIM) + EPS) * kg
        v = kv[:, KV_WIDTH:]
        low = lax.broadcasted_iota(jnp.int32, k.shape, 1) < HEAD_DIM
        for src, dst in ((k, k_ext), (v, v_ext)):
            swapped = pltpu.roll(src, HEAD_DIM, axis=1)
            for j in range(N_KV_HEADS):
                own, other = (src, swapped) if j == 0 else (swapped, src)
                dst[j, 0, rows, 0:PAIR] = jnp.where(low, own, 0.0).astype(BF16)
                dst[j, 1, rows, 0:PAIR] = jnp.where(low, 0.0, other).astype(BF16)
        for j in range(N_KV_HEADS):
            v_ext[j, 0, rows, PAIR:2 * PAIR] = jnp.where(low, 1.0, 0.0).astype(BF16)
            v_ext[j, 1, rows, PAIR:2 * PAIR] = jnp.where(low, 0.0, 1.0).astype(BF16)

    zq = _dot(hb, win_ref[0, :, COL_Q:COL_K])
    qg = qg_ref[0] * (HEAD_DIM ** -0.5 * LOG2E)
    seg = seg_ref[...]
    pairs_per_seg = SEG_WIDTH // PAIR
    for half in range(Q_WIDTH // SEG_WIDTH):
        cols = slice(half * SEG_WIDTH, (half + 1) * SEG_WIDTH)
        zh = zq[:, cols]
        qn = (zh * lax.rsqrt(_head_sumsq(zh, seg) * (1.0 / HEAD_DIM) + EPS) * qg[:, cols]).astype(BF16)
        for pp in range(pairs_per_seg):
            j, p = divmod(half * pairs_per_seg + pp, 2)
            for n in range(n_blk):
                q_buf[j, n, p * BLOCK:(p + 1) * BLOCK, :] = (
                    qn[n * BLOCK:(n + 1) * BLOCK, pp * PAIR:(pp + 1) * PAIR])

    uv = _dot(hb, win_ref[0, :, COL_U:COL_Q])
    vg = vg_ref[0]
    low_lane = lax.broadcasted_iota(jnp.int32, (BLOCK, PAIR), 1) < HEAD_DIM

    def attention_unit(n, j):
        band = slice(n * BLOCK, (n + 3) * BLOCK)
        gb = i * n_blk + n
        tix = jnp.where(gb == 0, 0, jnp.where(gb == n_seq_blocks - 1, 2, 1))
        q = q_buf[j, n]
        k = jnp.concatenate([k_ext[j, 0, band, :], k_ext[j, 1, band, :]], axis=0)
        s = lax.dot_general(q, k, (((1,), (1,)), ((), ())), preferred_element_type=F32)
        s = s + bias_ref[tix, j]
        sink_terms = []
        for p in range(2):
            for par in range(2):
                sg = s[p * BLOCK:(p + 1) * BLOCK, par * 3 * BLOCK:(par + 1) * 3 * BLOCK]
                sink = sink_ref[layer, j * Q_PER_KV + 2 * p + par] * LOG2E
                m = jnp.maximum(jnp.max(sg, axis=-1, keepdims=True), sink)
                e_buf[j, p * BLOCK:(p + 1) * BLOCK, par * 3 * BLOCK:(par + 1) * 3 * BLOCK] = (
                    jnp.exp2(sg - m).astype(BF16))
                sink_terms.append(jnp.exp2(sink - m))
        v = jnp.concatenate([v_ext[j, 0, band, :], v_ext[j, 1, band, :]], axis=0)
        o = _dot(e_buf[j], v)
        for p in range(2):
            rows = slice(p * BLOCK, (p + 1) * BLOCK)
            den = o[rows, PAIR:] + jnp.where(low_lane, sink_terms[2 * p], sink_terms[2 * p + 1])
            att_buf[n * BLOCK:(n + 1) * BLOCK, (2 * j + p) * PAIR:(2 * j + p + 1) * PAIR] = (
                o[rows, :PAIR] / den).astype(BF16)

    def gmlp_chunk(c):
        rows = slice(c * CHUNK, (c + 1) * CHUNK)
        u = jax.nn.gelu(uv[rows, :GM_WIDTH])
        v = _rms(jax.nn.gelu(uv[rows, GM_WIDTH:]), vg).astype(BF16)
        mixed = jnp.concatenate(
            [_dot(ws_ref[0, g], v[:, g * GM_GROUP_DIM:(g + 1) * GM_GROUP_DIM]) for g in range(GM_GROUPS)],
            axis=1)
        a_buf[rows, :] = (u * (mixed + bs_ref[0])).astype(BF16)

    for unit in range(n_units):
        n, j = divmod(unit, N_KV_HEADS)
        attention_unit(n, j)
        c0 = unit * gate_chunk
        gate_buf[:, c0:c0 + gate_chunk] = _dot(hb, win_ref[0, :, COL_GA + c0:COL_GA + c0 + gate_chunk])
        if unit < n_blk:
            gmlp_chunk(unit)
        else:
            c1 = (unit - n_blk) * gate_chunk
            gate_buf[:, c1:c1 + gate_chunk] = jax.nn.sigmoid(gate_buf[:, c1:c1 + gate_chunk])

    merged = gate_buf[:, :D_MODEL] * _dot(a_buf[...], wa_ref[0])
    merged = merged + jax.nn.sigmoid(gate_buf[:, D_MODEL:]) * _dot(att_buf[...], wb_ref[0])
    o_ref[0] = x + gt1 * _dot(merged.astype(BF16), wo_ref[0])


def _mixer(layer, x, mod, g1, w_in, vg, ws, bs_full, qg, kg, sink, bias_tbl, seg, w_a, w_b, w_o):
    batch, seq, d = x.shape
    tile = MIX_TILE
    n_blk = tile // BLOCK
    n_seq_blocks = seq // BLOCK
    kern = functools.partial(_mixer_kernel, tile=tile, n_seq_blocks=n_seq_blocks, layer=layer)
    return pl.pallas_call(
        kern,
        grid=(batch, seq // tile),
        in_specs=[
            pl.BlockSpec((1, tile, d), lambda b, i: (b, i, 0)),
            pl.BlockSpec((1, BLOCK, d), lambda b, i: (b, jnp.maximum(i * n_blk - 1, 0), 0)),
            pl.BlockSpec((1, BLOCK, d), lambda b, i: (b, jnp.minimum((i + 1) * n_blk, n_seq_blocks - 1), 0)),
            pl.BlockSpec((1, 1, N_MOD, d), lambda b, i: (layer, b, 0, 0)),
            _layer_spec(g1.shape, layer),
            _layer_spec(w_in.shape, layer),
            _layer_spec(vg.shape, layer),
            _layer_spec(ws.shape, layer),
            _layer_spec(bs_full.shape, layer),
            _layer_spec(qg.shape, layer),
            _layer_spec(kg.shape, layer),
            pl.BlockSpec(memory_space=pltpu.SMEM),
            _const_spec(bias_tbl.shape),
            _const_spec(seg.shape),
            _layer_spec(w_a.shape, layer),
            _layer_spec(w_b.shape, layer),
            _layer_spec(w_o.shape, layer),
        ],
        out_specs=pl.BlockSpec((1, tile, d), lambda b, i: (b, i, 0)),
        out_shape=jax.ShapeDtypeStruct(x.shape, x.dtype),
        scratch_shapes=[
            pltpu.VMEM((N_KV_HEADS, n_blk, 2 * BLOCK, PAIR), BF16),
            pltpu.VMEM((N_KV_HEADS, 2, tile + 2 * BLOCK, PAIR), BF16),
            pltpu.VMEM((N_KV_HEADS, 2, tile + 2 * BLOCK, 2 * PAIR), BF16),
            pltpu.VMEM((N_KV_HEADS, 2 * BLOCK, 6 * BLOCK), BF16),
            pltpu.VMEM((tile, GM_WIDTH), BF16),
            pltpu.VMEM((tile, Q_WIDTH), BF16),
            pltpu.VMEM((tile, 2 * D_MODEL), F32),
        ],
        compiler_params=pltpu.CompilerParams(
            dimension_semantics=("arbitrary", "arbitrary"), vmem_limit_bytes=VMEM_LIMIT_BYTES),
        name="token_mixer",
    )(x, x, x, mod, g1, w_in, vg, ws, bs_full, qg, kg, sink, bias_tbl, seg, w_a, w_b, w_o)


def _ffn_kernel(x_ref, mod_ref, g2_ref, w1_ref, w2_ref, o_ref):
    sh2 = mod_ref[0, 0, 3:4, :]
    sc2 = mod_ref[0, 0, 4:5, :]
    gt2 = mod_ref[0, 0, 5:6, :]
    x = x_ref[0]
    hb = (_rms(x, g2_ref[0]) * (1.0 + sc2) + sh2).astype(BF16)
    acc = jnp.zeros(x.shape, F32)
    for s in range(0, D_FF, FFN_CHUNK):
        f_gate = _dot(hb, w1_ref[0, :, s:s + FFN_CHUNK])
        f_up = _dot(hb, w1_ref[0, :, D_FF + s:D_FF + s + FFN_CHUNK])
        act = (f_gate * jax.nn.sigmoid(f_gate) * f_up).astype(BF16)
        acc = acc + _dot(act, w2_ref[0, s:s + FFN_CHUNK, :])
    o_ref[0] = x + gt2 * acc


def _ffn(layer, x, mod, g2, w1, w2):
    batch, seq, d = x.shape
    tile = FFN_TILE
    return pl.pallas_call(
        _ffn_kernel,
        grid=(batch, seq // tile),
        in_specs=[
            pl.BlockSpec((1, tile, d), lambda b, i: (b, i, 0)),
            pl.BlockSpec((1, 1, N_MOD, d), lambda b, i: (layer, b, 0, 0)),
            _layer_spec(g2.shape, layer),
            _layer_spec(w1.shape, layer),
            _layer_spec(w2.shape, layer),
        ],
        out_specs=pl.BlockSpec((1, tile, d), lambda b, i: (b, i, 0)),
        out_shape=jax.ShapeDtypeStruct(x.shape, x.dtype),
        compiler_params=pltpu.CompilerParams(
            dimension_semantics=("arbitrary", "arbitrary"), vmem_limit_bytes=VMEM_LIMIT_BYTES),
        name="swiglu_ffn",
    )(x, mod, g2, w1, w2)


def kernel(x, c, w_ada, b_ada, norm1_g, w_in, gm_v_g, gm_w_s, gm_b_s, q_norm_g, k_norm_g, attn_sink,
           w_a, w_b, w_o, norm2_g, w_ffn_in, w_ffn_out):
    n_layers = w_ada.shape[0]
    batch = x.shape[0]
    assert x.shape[1] % MIX_TILE == 0 and x.shape[1] % FFN_TILE == 0 and D_FF % FFN_CHUNK == 0

    mod = _ada_mod(c, w_ada, b_ada).reshape(n_layers, batch, N_MOD, D_MODEL)
    bias_tbl = jnp.asarray(_attn_bias_tables())
    seg = jnp.asarray(_segment_ones(SEG_WIDTH), dtype=BF16)
    bs_full = jnp.repeat(jnp.swapaxes(gm_b_s, 1, 2), GM_GROUP_DIM, axis=2)
    qg = jnp.tile(q_norm_g, (1, N_HEADS))[:, None, :]
    kg = jnp.tile(k_norm_g, (1, N_KV_HEADS))[:, None, :]
    g1, g2, vg = norm1_g[:, None, :], norm2_g[:, None, :], gm_v_g[:, None, :]
    w_in, ws, w_a, w_b, w_o, w1, w2 = (
        w.astype(BF16) for w in (w_in, gm_w_s, w_a, w_b, w_o, w_ffn_in, w_ffn_out))

    for l in range(n_layers):
        x = _mixer(l, x, mod, g1, w_in, vg, ws, bs_full, qg, kg, attn_sink, bias_tbl, seg, w_a, w_b, w_o)
        x = _ffn(l, x, mod, g2, w1, w2)
    return x
```

```python
import functools
import math

import numpy as np
import jax
import jax.numpy as jnp
from jax import lax
from jax.experimental import pallas as pl
from jax.experimental.pallas import tpu as pltpu

D_MODEL = 1024
CHUNK = 128
GM_GROUPS = 8
GM_WIDTH = 512
GM_GROUP_DIM = GM_WIDTH // GM_GROUPS
N_HEADS = 8
N_KV_HEADS = 2
HEAD_DIM = 64
Q_PER_KV = N_HEADS // N_KV_HEADS
Q_WIDTH = N_HEADS * HEAD_DIM
KV_WIDTH = N_KV_HEADS * HEAD_DIM
BLOCK = 128
D_FF = ((-(-8 * D_MODEL // 3) + 255) // 256) * 256
N_MOD = 6
EPS = 1e-6
NEG_INF = -1e30
LOG2E = math.log2(math.e)

COL_U = 0
COL_Q = 2 * GM_WIDTH
COL_K = COL_Q + Q_WIDTH
COL_V = COL_K + KV_WIDTH
COL_GA = COL_V + KV_WIDTH
COL_GB = COL_GA + D_MODEL
IN_WIDTH = COL_GB + D_MODEL

PAIR = 2 * HEAD_DIM
SEG_WIDTH = 256
MIX_TILE = 512
FFN_TILE = 512
FFN_CHUNK = 256
ADA_TILE = 1536
VMEM_LIMIT_BYTES = 56 * 1024 * 1024

BF16 = jnp.bfloat16
F32 = jnp.float32


def _rms(x, gain):
    ms = jnp.mean(x * x, axis=-1, keepdims=True)
    return x * lax.rsqrt(ms + EPS) * gain


def _dot(a, b):
    return jnp.dot(a, b, preferred_element_type=F32)


def _const_spec(shape):
    zeros = (0,) * len(shape)
    return pl.BlockSpec(shape, lambda *_: zeros, pipeline_mode=pl.Buffered(1))


def _layer_spec(shape, layer):
    index = (layer,) + (0,) * (len(shape) - 1)
    return pl.BlockSpec((1,) + tuple(shape[1:]), lambda *_: index, pipeline_mode=pl.Buffered(1))


def _ada_kernel(c_ref, w_ref, b_ref, o_ref):
    c = c_ref[...]
    a = (c * jax.nn.sigmoid(c)).astype(BF16)
    o_ref[0] = _dot(a, w_ref[0].astype(BF16)) + b_ref[0]


def _ada_mod(c, w_ada, b_ada):
    n_layers, d, width = w_ada.shape
    batch = c.shape[0]
    return pl.pallas_call(
        _ada_kernel,
        grid=(n_layers, width // ADA_TILE),
        in_specs=[
            pl.BlockSpec((batch, d), lambda l, j: (0, 0)),
            pl.BlockSpec((1, d, ADA_TILE), lambda l, j: (l, 0, j)),
            pl.BlockSpec((1, 1, ADA_TILE), lambda l, j: (l, 0, j)),
        ],
        out_specs=pl.BlockSpec((1, batch, ADA_TILE), lambda l, j: (l, 0, j)),
        out_shape=jax.ShapeDtypeStruct((n_layers, batch, width), F32),
        compiler_params=pltpu.CompilerParams(
            dimension_semantics=("arbitrary", "arbitrary"), vmem_limit_bytes=VMEM_LIMIT_BYTES),
        name="adaln_mod",
    )(c, w_ada, b_ada.reshape(n_layers, 1, width))


def _attn_bias_tables():
    qi = np.arange(BLOCK)[:, None]
    kj = np.arange(3 * BLOCK)[None, :]
    dist = np.abs(kj - BLOCK - qi)
    slopes = np.exp2(-8.0 * np.arange(1, N_HEADS + 1) / N_HEADS).astype(np.float32)
    bias = -slopes[:, None, None] * dist[None].astype(np.float32) * np.float32(LOG2E)
    tables = []
    for lo, hi in ((BLOCK, 3 * BLOCK), (0, 3 * BLOCK), (0, 2 * BLOCK)):
        valid = (dist <= BLOCK) & (kj >= lo) & (kj < hi)
        tables.append(np.where(valid[None], bias, np.float32(NEG_INF)))
    t = np.stack(tables).astype(np.float32)
    t = t.reshape(3, N_KV_HEADS, 2, 2, BLOCK, 3 * BLOCK)
    return t.transpose(0, 1, 2, 4, 3, 5).reshape(3, N_KV_HEADS, 2 * BLOCK, 6 * BLOCK)


def _segment_ones(width):
    idx = np.arange(width) // HEAD_DIM
    return (idx[:, None] == idx[None, :]).astype(np.float32)


def _head_sumsq(x, seg):
    sq = x * x
    hi = sq.astype(BF16)
    lo = (sq - hi.astype(F32)).astype(BF16)
    return _dot(hi, seg) + _dot(lo, seg)


def _mixer_kernel(x_ref, xn_ref, xprev_ref, mod_ref, modprev_ref, g1_ref, win_ref, vg_ref, ws_ref, bs_ref,
                  qg_ref, kg_ref, sink_ref, bias_ref, seg_ref, wa_ref, wb_ref, wo_ref, o_ref,
                  hb_ext, merged_buf, q_buf, k_ext, v_ext, e_buf, a_buf, att_buf, gate_buf,
                  *, tile, n_tiles, n_steps, n_seq_blocks, layer):
    n_blk = tile // BLOCK
    n_units = N_KV_HEADS * n_blk
    gate_chunk = 2 * D_MODEL // n_units
    assert gate_chunk * n_blk == D_MODEL
    step = pl.program_id(0)
    i = lax.rem(jnp.minimum(step, n_steps - 1), n_tiles)
    sh1 = mod_ref[0, 0, 0:1, :]
    g1s = g1_ref[0] * (1.0 + mod_ref[0, 0, 1:2, :])

    def modnorm(xv):
        ms = jnp.mean(xv * xv, axis=-1, keepdims=True)
        return (xv * lax.rsqrt(ms + EPS) * g1s + sh1).astype(BF16)

    @pl.when(step == 0)
    def _():
        merged_buf[...] = jnp.zeros(merged_buf.shape, BF16)

    @pl.when(i == 0)
    def _():
        hb_ext[0:BLOCK, :] = jnp.zeros((BLOCK, D_MODEL), BF16)
        hb_ext[BLOCK:2 * BLOCK, :] = modnorm(x_ref[0, 0:BLOCK, :])
        k_ext[:, :, 0:BLOCK, :] = jnp.zeros((N_KV_HEADS, 2, BLOCK, PAIR), BF16)
        v_ext[:, :, 0:BLOCK, :] = jnp.zeros((N_KV_HEADS, 2, BLOCK, 2 * PAIR), BF16)

    @pl.when(i != 0)
    def _():
        hb_ext[0:2 * BLOCK, :] = hb_ext[tile:tile + 2 * BLOCK, :]
        k_ext[:, :, 0:BLOCK, :] = k_ext[:, :, tile:tile + BLOCK, :]
        v_ext[:, :, 0:BLOCK, :] = v_ext[:, :, tile:tile + BLOCK, :]

    y_prev = _dot(merged_buf[...], wo_ref[0])
    x = x_ref[0]
    hb_ext[2 * BLOCK:BLOCK + tile, :] = modnorm(x[BLOCK:, :])
    hb_ext[BLOCK + tile:, :] = modnorm(xn_ref[0])
    o_ref[0] = xprev_ref[0] + modprev_ref[0, 0, 2:3, :] * y_prev
    hb = hb_ext[BLOCK:BLOCK + tile, :]

    qkv = _dot(hb_ext[BLOCK:, :], win_ref[0, :, COL_Q:COL_GA])
    uv = _dot(hb, win_ref[0, :, COL_U:COL_Q])
    k = qkv[:, Q_WIDTH:Q_WIDTH + KV_WIDTH]
    k = k * lax.rsqrt(_head_sumsq(k, seg_ref[0:KV_WIDTH, 0:KV_WIDTH]) * (1.0 / HEAD_DIM) + EPS) * kg_ref[0]
    v = qkv[:, Q_WIDTH + KV_WIDTH:]
    low = lax.broadcasted_iota(jnp.int32, k.shape, 1) < HEAD_DIM
    for src, dst in ((k, k_ext), (v, v_ext)):
        swapped = pltpu.roll(src, HEAD_DIM, axis=1)
        for j in range(N_KV_HEADS):
            own, other = (src, swapped) if j == 0 else (swapped, src)
            dst[j, 0, BLOCK:, 0:PAIR] = jnp.where(low, own, 0.0).astype(BF16)
            dst[j, 1, BLOCK:, 0:PAIR] = jnp.where(low, 0.0, other).astype(BF16)
    for j in range(N_KV_HEADS):
        v_ext[j, 0, BLOCK:, PAIR:2 * PAIR] = jnp.where(low, 1.0, 0.0).astype(BF16)
        v_ext[j, 1, BLOCK:, PAIR:2 * PAIR] = jnp.where(low, 0.0, 1.0).astype(BF16)

    zq = qkv[0:tile, 0:Q_WIDTH]
    qg = qg_ref[0] * (HEAD_DIM ** -0.5 * LOG2E)
    seg = seg_ref[...]
    pairs_per_seg = SEG_WIDTH // PAIR
    for half in range(Q_WIDTH // SEG_WIDTH):
        cols = slice(half * SEG_WIDTH, (half + 1) * SEG_WIDTH)
        zh = zq[:, cols]
        qn = (zh * lax.rsqrt(_head_sumsq(zh, seg) * (1.0 / HEAD_DIM) + EPS) * qg[:, cols]).astype(BF16)
        for pp in range(pairs_per_seg):
            j, p = divmod(half * pairs_per_seg + pp, 2)
            for n in range(n_blk):
                q_buf[j, n, p * BLOCK:(p + 1) * BLOCK, :] = (
                    qn[n * BLOCK:(n + 1) * BLOCK, pp * PAIR:(pp + 1) * PAIR])

    vg = vg_ref[0]
    low_lane =lax.broadcasted_iota(jnp.int32, (BLOCK, PAIR), 1) < HEAD_DIM

    def unit_logits(n, j):
        band = slice(n * BLOCK, (n + 3) * BLOCK)
        gb = i * n_blk + n
        tix = jnp.where(gb == 0, 0, jnp.where(gb == n_seq_blocks - 1, 2, 1))
        q = q_buf[j, n]
        k = jnp.concatenate([k_ext[j, 0, band, :], k_ext[j, 1, band, :]], axis=0)
        s = lax.dot_general(q, k, (((1,), (1,)), ((), ())), preferred_element_type=F32)
        return s + bias_ref[tix, j]

    def unit_softmax_pv(n, j, s):
        band = slice(n * BLOCK, (n + 3) * BLOCK)
        sink_terms = []
        for p in range(2):
            for par in range(2):
                sg = s[p * BLOCK:(p + 1) * BLOCK, par * 3 * BLOCK:(par + 1) * 3 * BLOCK]
                sink = sink_ref[layer, j * Q_PER_KV + 2 * p + par] * LOG2E
                m = jnp.maximum(jnp.max(sg, axis=-1, keepdims=True), sink)
                e_buf[j, p * BLOCK:(p + 1) * BLOCK, par * 3 * BLOCK:(par + 1) * 3 * BLOCK] = (
                    jnp.exp2(sg - m).astype(BF16))
                sink_terms.append(jnp.exp2(sink - m))
        v = jnp.concatenate([v_ext[j, 0, band, :], v_ext[j, 1, band, :]], axis=0)
        o = _dot(e_buf[j], v)
        for p in range(2):
            rows = slice(p * BLOCK, (p + 1) * BLOCK)
            den = o[rows, PAIR:] + jnp.where(low_lane, sink_terms[2 * p], sink_terms[2 * p + 1])
            att_buf[n * BLOCK:(n + 1) * BLOCK, (2 * j + p) * PAIR:(2 * j + p + 1) * PAIR] = (
                o[rows, :PAIR] / den).astype(BF16)

    def gmlp_chunk(c):
        rows = slice(c * CHUNK, (c + 1) * CHUNK)
        u = jax.nn.gelu(uv[rows, :GM_WIDTH])
        v = _rms(jax.nn.gelu(uv[rows, GM_WIDTH:]), vg).astype(BF16)
        mixed = jnp.concatenate(
            [_dot(ws_ref[0, g], v[:, g * GM_GROUP_DIM:(g + 1) * GM_GROUP_DIM]) for g in range(GM_GROUPS)],
            axis=1)
        a_buf[rows, :] = (u * (mixed + bs_ref[0])).astype(BF16)

    s_next = unit_logits(0, 0)
    for unit in range(n_units):
        n, j = divmod(unit, N_KV_HEADS)
        s_cur = s_next
        if unit + 1 < n_units:
            s_next = unit_logits(*divmod(unit + 1, N_KV_HEADS))
        if unit < n_blk:
            gmlp_chunk(unit)
        else:
            c1 = (unit - n_blk) * gate_chunk
            gate_buf[:, c1:c1 + gate_chunk] = jax.nn.sigmoid(gate_buf[:, c1:c1 + gate_chunk])
        unit_softmax_pv(n, j, s_cur)
        if unit % 2 == 0:
            c0, c2 = unit * gate_chunk, (unit + 2) * gate_chunk
            gate_buf[:, c0:c2] = _dot(hb, win_ref[0, :, COL_GA + c0:COL_GA + c2])

    merged = gate_buf[:, :D_MODEL] * _dot(a_buf[...], wa_ref[0])
    merged = merged + jax.nn.sigmoid(gate_buf[:, D_MODEL:]) * _dot(att_buf[...], wb_ref[0])
    merged_buf[...] = merged.astype(BF16)


def _mixer(layer, x, mod, g1, w_in, vg, ws, bs_full, qg, kg, sink, bias_tbl, seg, w_a, w_b, w_o):
    batch, seq, d = x.shape
    tile = MIX_TILE
    n_blk = tile // BLOCK
    n_seq_blocks = seq // BLOCK
    n_tiles = seq // tile
    n_steps = batch * n_tiles
    kern = functools.partial(_mixer_kernel, tile=tile, n_tiles=n_tiles, n_steps=n_steps,
                             n_seq_blocks=n_seq_blocks, layer=layer)

    def cur(s):
        t = jnp.minimum(s, n_steps - 1)
        return t // n_tiles, t % n_tiles

    def prev(s):
        t = jnp.maximum(s - 1, 0)
        return t // n_tiles, t % n_tiles

    return pl.pallas_call(
        kern,
        grid=(n_steps + 1,),
        in_specs=[
            pl.BlockSpec((1, tile, d), lambda s: (*cur(s), 0)),
            pl.BlockSpec((1, BLOCK, d), lambda s: (
                cur(s)[0], jnp.minimum((cur(s)[1] + 1) * n_blk, n_seq_blocks - 1), 0)),
            pl.BlockSpec((1, tile, d), lambda s: (*prev(s), 0)),
            pl.BlockSpec((1, 1, N_MOD, d), lambda s: (layer, cur(s)[0], 0, 0)),
            pl.BlockSpec((1, 1, N_MOD, d), lambda s: (layer, prev(s)[0], 0, 0)),
            _layer_spec(g1.shape, layer),
            _layer_spec(w_in.shape, layer),
            _layer_spec(vg.shape, layer),
            _layer_spec(ws.shape, layer),
            _layer_spec(bs_full.shape, layer),
            _layer_spec(qg.shape, layer),
            _layer_spec(kg.shape, layer),
            pl.BlockSpec(memory_space=pltpu.SMEM),
            _const_spec(bias_tbl.shape),
            _const_spec(seg.shape),
            _layer_spec(w_a.shape, layer),
            _layer_spec(w_b.shape, layer),
            _layer_spec(w_o.shape, layer),
        ],
        out_specs=pl.BlockSpec((1, tile, d), lambda s: (*prev(s), 0)),
        out_shape=jax.ShapeDtypeStruct(x.shape, x.dtype),
        scratch_shapes=[
            pltpu.VMEM((tile + 2 * BLOCK, d), BF16),
            pltpu.VMEM((tile, d), BF16),
            pltpu.VMEM((N_KV_HEADS, n_blk, 2 * BLOCK, PAIR), BF16),
            pltpu.VMEM((N_KV_HEADS, 2, tile + 2 * BLOCK, PAIR), BF16),
            pltpu.VMEM((N_KV_HEADS, 2, tile + 2 * BLOCK, 2 * PAIR), BF16),
            pltpu.VMEM((N_KV_HEADS, 2 * BLOCK, 6 * BLOCK), BF16),
            pltpu.VMEM((tile, GM_WIDTH), BF16),
            pltpu.VMEM((tile, Q_WIDTH), BF16),
            pltpu.VMEM((tile, 2 * D_MODEL), F32),
        ],
        compiler_params=pltpu.CompilerParams(
            dimension_semantics=("arbitrary",), vmem_limit_bytes=VMEM_LIMIT_BYTES),
        name="token_mixer",
    )(x, x, x, mod, mod, g1, w_in, vg, ws, bs_full, qg, kg, sink, bias_tbl, seg, w_a, w_b, w_o)


def _ffn_kernel(x_ref, mod_ref, g2_ref, w1_ref, w2_ref, o_ref):
    sh2 = mod_ref[0, 0, 3:4, :]
    sc2 = mod_ref[0, 0, 4:5, :]
    gt2 = mod_ref[0, 0, 5:6, :]
    x = x_ref[0]
    hb = (_rms(x, g2_ref[0]) * (1.0 + sc2) + sh2).astype(BF16)
    acc = jnp.zeros(x.shape, F32)
    for s in range(0, D_FF, FFN_CHUNK):
        f_gate = _dot(hb, w1_ref[0, :, s:s + FFN_CHUNK])
        f_up = _dot(hb, w1_ref[0, :, D_FF + s:D_FF + s + FFN_CHUNK])
        act = (f_gate * jax.nn.sigmoid(f_gate) * f_up).astype(BF16)
        acc = acc + _dot(act, w2_ref[0, s:s + FFN_CHUNK, :])
    o_ref[0] = x + gt2 * acc


def _ffn(layer, x, mod, g2, w1, w2):
    batch, seq, d = x.shape
    tile = FFN_TILE
    return pl.pallas_call(
        _ffn_kernel,
        grid=(batch, seq // tile),
        in_specs=[
            pl.BlockSpec((1, tile, d), lambda b, i: (b, i, 0)),
            pl.BlockSpec((1, 1, N_MOD, d), lambda b, i: (layer, b, 0, 0)),
            _layer_spec(g2.shape, layer),
            _layer_spec(w1.shape, layer),
            _layer_spec(w2.shape, layer),
        ],
        out_specs=pl.BlockSpec((1, tile, d), lambda b, i: (b, i, 0)),
        out_shape=jax.ShapeDtypeStruct(x.shape, x.dtype),
        compiler_params=pltpu.CompilerParams(
            dimension_semantics=("arbitrary", "arbitrary"), vmem_limit_bytes=VMEM_LIMIT_BYTES),
        name="swiglu_ffn",
    )(x, mod, g2, w1, w2)


def kernel(x, c, w_ada, b_ada, norm1_g, w_in, gm_v_g, gm_w_s, gm_b_s, q_norm_g, k_norm_g, attn_sink,
           w_a, w_b, w_o, norm2_g, w_ffn_in, w_ffn_out):
    n_layers = w_ada.shape[0]
    batch = x.shape[0]
    assert x.shape[1] % MIX_TILE == 0 and x.shape[1] % FFN_TILE == 0 and D_FF % FFN_CHUNK == 0

    mod = _ada_mod(c, w_ada, b_ada).reshape(n_layers, batch, N_MOD, D_MODEL)
    bias_tbl = jnp.asarray(_attn_bias_tables())
    seg = jnp.asarray(_segment_ones(SEG_WIDTH), dtype=BF16)
    bs_full = jnp.repeat(jnp.swapaxes(gm_b_s, 1, 2), GM_GROUP_DIM, axis=2)
    qg = jnp.tile(q_norm_g, (1, N_HEADS))[:, None, :]
    kg = jnp.tile(k_norm_g, (1, N_KV_HEADS))[:, None, :]
    g1, g2, vg = norm1_g[:, None, :], norm2_g[:, None, :], gm_v_g[:, None, :]
    w_in, ws, w_a, w_b, w_o, w1, w2 = (
        w.astype(BF16) for w in (w_in, gm_w_s, w_a, w_b, w_o, w_ffn_in, w_ffn_out))

    for l in range(n_layers):
        x = _mixer(l, x, mod, g1, w_in, vg, ws, bs_full, qg, kg, attn_sink, bias_tbl, seg, w_a, w_b, w_o)
        x = _ffn(l, x, mod, g2, w1, w2)
    return x
```

```python
import functools
import math

import numpy as np
import jax
import jax.numpy as jnp
from jax import lax
from jax.experimental import pallas as pl
from jax.experimental.pallas import tpu as pltpu

D_MODEL = 1024
CHUNK = 128
GM_GROUPS = 8
GM_WIDTH = 512
GM_GROUP_DIM = GM_WIDTH // GM_GROUPS
N_HEADS = 8
N_KV_HEADS = 2
HEAD_DIM = 64
Q_PER_KV = N_HEADS // N_KV_HEADS
Q_WIDTH = N_HEADS * HEAD_DIM
KV_WIDTH = N_KV_HEADS * HEAD_DIM
BLOCK = 128
D_FF = ((-(-8 * D_MODEL // 3) + 255) // 256) * 256
N_MOD = 6
EPS = 1e-6
NEG_INF = -1e30
LOG2E = math.log2(math.e)

COL_U = 0
COL_Q = 2 * GM_WIDTH
COL_K = COL_Q + Q_WIDTH
COL_V = COL_K + KV_WIDTH
COL_GA = COL_V + KV_WIDTH
COL_GB = COL_GA + D_MODEL
IN_WIDTH = COL_GB + D_MODEL

PAIR = 2 * HEAD_DIM
SEG_WIDTH = 256
MIX_TILE = 512
FFN_TILE = 1024
FFN_CHUNK = 256
ADA_TILE = 1536
VMEM_LIMIT_BYTES = 56 * 1024 * 1024

BF16 = jnp.bfloat16
F32 = jnp.float32


def _rms(x, gain):
    ms = jnp.mean(x * x, axis=-1, keepdims=True)
    return x * lax.rsqrt(ms + EPS) * gain


def _dot(a, b):
    return jnp.dot(a, b, preferred_element_type=F32)


def _const_spec(shape):
    zeros = (0,) * len(shape)
    return pl.BlockSpec(shape, lambda *_: zeros, pipeline_mode=pl.Buffered(1))


def _layer_spec(shape, layer):
    index = (layer,) + (0,) * (len(shape) - 1)
    return pl.BlockSpec((1,) + tuple(shape[1:]), lambda *_: index, pipeline_mode=pl.Buffered(1))


def _ada_kernel(c_ref, w_ref, b_ref, o_ref):
    c = c_ref[...]
    a = (c * jax.nn.sigmoid(c)).astype(BF16)
    o_ref[0] = _dot(a, w_ref[0].astype(BF16)) + b_ref[0]


def _ada_mod(c, w_ada, b_ada):
    n_layers, d, width = w_ada.shape
    batch = c.shape[0]
    return pl.pallas_call(
        _ada_kernel,
        grid=(n_layers, width // ADA_TILE),
        in_specs=[
            pl.BlockSpec((batch, d), lambda l, j: (0, 0)),
            pl.BlockSpec((1, d, ADA_TILE), lambda l, j: (l, 0, j)),
            pl.BlockSpec((1, 1, ADA_TILE), lambda l, j: (l, 0, j)),
        ],
        out_specs=pl.BlockSpec((1, batch, ADA_TILE), lambda l, j: (l, 0, j)),
        out_shape=jax.ShapeDtypeStruct((n_layers, batch, width), F32),
        compiler_params=pltpu.CompilerParams(
            dimension_semantics=("arbitrary", "arbitrary"), vmem_limit_bytes=VMEM_LIMIT_BYTES),
        name="adaln_mod",
    )(c, w_ada, b_ada.reshape(n_layers, 1, width))


def _attn_bias_tables():
    qi = np.arange(BLOCK)[:, None]
    kj = np.arange(3 * BLOCK)[None, :]
    dist = np.abs(kj - BLOCK - qi)
    slopes = np.exp2(-8.0 * np.arange(1, N_HEADS + 1) / N_HEADS).astype(np.float32)
    bias = -slopes[:, None, None] * dist[None].astype(np.float32) * np.float32(LOG2E)
    tables = []
    for lo, hi in ((BLOCK, 3 * BLOCK), (0, 3 * BLOCK), (0, 2 * BLOCK)):
        valid = (dist <= BLOCK) & (kj >= lo) & (kj < hi)
        tables.append(np.where(valid[None], bias, np.float32(NEG_INF)))
    t = np.stack(tables).astype(np.float32)
    t = t.reshape(3, N_KV_HEADS, 2, 2, BLOCK, 3 * BLOCK)
    return t.transpose(0, 1, 2, 4, 3, 5).reshape(3, N_KV_HEADS, 2 * BLOCK, 6 * BLOCK)


def _segment_ones(width):
    idx = np.arange(width) // HEAD_DIM
    return (idx[:, None] == idx[None, :]).astype(np.float32)


def _head_sumsq(x, seg):
    sq = x * x
    hi = sq.astype(BF16)
    lo = (sq - hi.astype(F32)).astype(BF16)
    return _dot(hi, seg) + _dot(lo, seg)


def _mixer_kernel(x_ref, xn_ref, xprev_ref, mod_ref, modprev_ref, g1_ref, win_ref, vg_ref, ws_ref, bs_ref,
                  qg_ref, kg_ref, sink_ref, bias_ref, seg_ref, wa_ref, wb_ref, wo_ref, o_ref,
                  hb_ext, merged_buf, q_buf, k_ext, v_ext, e_buf, a_buf, att_buf, gate_buf,
                  *, tile, n_tiles, n_steps, n_seq_blocks, layer):
    n_blk = tile // BLOCK
    n_units = N_KV_HEADS * n_blk
    gate_chunk = 2 * D_MODEL // n_units
    assert gate_chunk * n_blk == D_MODEL
    step = pl.program_id(0)
    i = lax.rem(jnp.minimum(step, n_steps - 1), n_tiles)
    sh1 = mod_ref[0, 0, 0:1, :]
    g1s = g1_ref[0] * (1.0 + mod_ref[0, 0, 1:2, :])

    def modnorm(xv):
        ms = jnp.mean(xv * xv, axis=-1, keepdims=True)
        return (xv * lax.rsqrt(ms + EPS) * g1s + sh1).astype(BF16)

    @pl.when(step == 0)
    def _():
        merged_buf[...] = jnp.zeros(merged_buf.shape, BF16)

    @pl.when(i == 0)
    def _():
        hb_ext[0:BLOCK, :] = jnp.zeros((BLOCK, D_MODEL), BF16)
        hb_ext[BLOCK:2 * BLOCK, :] = modnorm(x_ref[0, 0:BLOCK, :])
        k_ext[:, :, 0:BLOCK, :] = jnp.zeros((N_KV_HEADS, 2, BLOCK, PAIR), BF16)
        v_ext[:, :, 0:BLOCK, :] = jnp.zeros((N_KV_HEADS, 2, BLOCK, 2 * PAIR), BF16)

    @pl.when(i != 0)
    def _():
        hb_ext[0:2 * BLOCK, :] = hb_ext[tile:tile + 2 * BLOCK, :]
        k_ext[:, :, 0:BLOCK, :] = k_ext[:, :, tile:tile + BLOCK, :]
        v_ext[:, :, 0:BLOCK, :] = v_ext[:, :, tile:tile + BLOCK, :]

    y_prev = _dot(merged_buf[...], wo_ref[0])
    x = x_ref[0]
    hb_ext[2 * BLOCK:BLOCK + tile, :] = modnorm(x[BLOCK:, :])
    hb_ext[BLOCK + tile:, :] = modnorm(xn_ref[0])
    o_ref[0] = xprev_ref[0] + modprev_ref[0, 0, 2:3, :] * y_prev
    hb = hb_ext[BLOCK:BLOCK + tile, :]

    qkv = _dot(hb_ext[BLOCK:, :], win_ref[0, :, COL_Q:COL_GA])
    uv = _dot(hb, win_ref[0, :, COL_U:COL_Q])
    k = qkv[:, Q_WIDTH:Q_WIDTH + KV_WIDTH]
    k = k * lax.rsqrt(_head_sumsq(k, seg_ref[0:KV_WIDTH, 0:KV_WIDTH]) * (1.0 / HEAD_DIM) + EPS) * kg_ref[0]
    v = qkv[:, Q_WIDTH + KV_WIDTH:]
    low = lax.broadcasted_iota(jnp.int32, k.shape, 1) < HEAD_DIM
    for src, dst in ((k, k_ext), (v, v_ext)):
        swapped = pltpu.roll(src, HEAD_DIM, axis=1)
        for j in range(N_KV_HEADS):
            own, other = (src, swapped) if j == 0 else (swapped, src)
            dst[j, 0, BLOCK:, 0:PAIR] = jnp.where(low, own, 0.0).astype(BF16)
            dst[j, 1, BLOCK:, 0:PAIR] = jnp.where(low, 0.0, other).astype(BF16)
    for j in range(N_KV_HEADS):
        v_ext[j, 0, BLOCK:, PAIR:2 * PAIR] = jnp.where(low, 1.0, 0.0).astype(BF16)
        v_ext[j, 1, BLOCK:, PAIR:2 * PAIR] = jnp.where(low, 0.0, 1.0).astype(BF16)

    zq = qkv[0:tile, 0:Q_WIDTH]
    qg = qg_ref[0] * (HEAD_DIM ** -0.5 * LOG2E)
    seg = seg_ref[...]
    pairs_per_seg = SEG_WIDTH // PAIR
    for half in range(Q_WIDTH // SEG_WIDTH):
        cols = slice(half * SEG_WIDTH, (half + 1) * SEG_WIDTH)
        zh = zq[:, cols]
        qn = (zh * lax.rsqrt(_head_sumsq(zh, seg) * (1.0 / HEAD_DIM) + EPS) * qg[:, cols]).astype(BF16)
        for pp in range(pairs_per_seg):
            j, p = divmod(half * pairs_per_seg + pp, 2)
            for n in range(n_blk):
                q_buf[j, n, p * BLOCK:(p + 1) * BLOCK, :] = (
                    qn[n * BLOCK:(n + 1) * BLOCK, pp * PAIR:(pp + 1) * PAIR])

    vg = vg_ref[0]
    low_lane =lax.broadcasted_iota(jnp.int32, (BLOCK, PAIR), 1) < HEAD_DIM

    def unit_logits(n, j):
        band = slice(n * BLOCK, (n + 3) * BLOCK)
        gb = i * n_blk + n
        tix = jnp.where(gb == 0, 0, jnp.where(gb == n_seq_blocks - 1, 2, 1))
        q = q_buf[j, n]
        k = jnp.concatenate([k_ext[j, 0, band, :], k_ext[j, 1, band, :]], axis=0)
        s = lax.dot_general(q, k, (((1,), (1,)), ((), ())), preferred_element_type=F32)
        return s + bias_ref[tix, j]

    def unit_softmax_pv(n, j, s):
        band = slice(n * BLOCK, (n + 3) * BLOCK)
        sink_terms = []
        for p in range(2):
            for par in range(2):
                sg = s[p * BLOCK:(p + 1) * BLOCK, par * 3 * BLOCK:(par + 1) * 3 * BLOCK]
                sink = sink_ref[layer, j * Q_PER_KV + 2 * p + par] * LOG2E
                m = jnp.maximum(jnp.max(sg, axis=-1, keepdims=True), sink)
                e_buf[j, p * BLOCK:(p + 1) * BLOCK, par * 3 * BLOCK:(par + 1) * 3 * BLOCK] = (
                    jnp.exp2(sg - m).astype(BF16))
                sink_terms.append(jnp.exp2(sink - m))
        v = jnp.concatenate([v_ext[j, 0, band, :], v_ext[j, 1, band, :]], axis=0)
        o = _dot(e_buf[j], v)
        for p in range(2):
            rows = slice(p * BLOCK, (p + 1) * BLOCK)
            den = o[rows, PAIR:] + jnp.where(low_lane, sink_terms[2 * p], sink_terms[2 * p + 1])
            att_buf[n * BLOCK:(n + 1) * BLOCK, (2 * j + p) * PAIR:(2 * j + p + 1) * PAIR] = (
                o[rows, :PAIR] / den).astype(BF16)

    def gmlp_chunk(c):
        rows = slice(c * CHUNK, (c + 1) * CHUNK)
        u = jax.nn.gelu(uv[rows, :GM_WIDTH])
        v = _rms(jax.nn.gelu(uv[rows, GM_WIDTH:]), vg).astype(BF16)
        mixed = jnp.concatenate(
            [_dot(ws_ref[0, g], v[:, g * GM_GROUP_DIM:(g + 1) * GM_GROUP_DIM]) for g in range(GM_GROUPS)],
            axis=1)
        a_buf[rows, :] = (u * (mixed + bs_ref[0])).astype(BF16)

    s_next = unit_logits(0, 0)
    for unit in range(n_units):
        n, j = divmod(unit, N_KV_HEADS)
        s_cur = s_next
        if unit + 1 < n_units:
            s_next = unit_logits(*divmod(unit + 1, N_KV_HEADS))
        if unit < n_blk:
            gmlp_chunk(unit)
        else:
            c1 = (unit - n_blk) * gate_chunk
            gate_buf[:, c1:c1 + gate_chunk] = jax.nn.sigmoid(gate_buf[:, c1:c1 + gate_chunk])
        unit_softmax_pv(n, j, s_cur)
        if unit % 2 == 0:
            c0, c2 = unit * gate_chunk, (unit + 2) * gate_chunk
            gate_buf[:, c0:c2] = _dot(hb, win_ref[0, :, COL_GA + c0:COL_GA + c2])

    merged = gate_buf[:, :D_MODEL] * _dot(a_buf[...], wa_ref[0])
    merged = merged + jax.nn.sigmoid(gate_buf[:, D_MODEL:]) * _dot(att_buf[...], wb_ref[0])
    merged_buf[...] = merged.astype(BF16)


def _mixer(layer, x, mod, g1, w_in, vg, ws, bs_full, qg, kg, sink, bias_tbl, seg, w_a, w_b, w_o):
    batch, seq, d = x.shape
    tile = MIX_TILE
    n_blk = tile // BLOCK
    n_seq_blocks = seq // BLOCK
    n_tiles = seq // tile
    n_steps = batch * n_tiles
    kern = functools.partial(_mixer_kernel, tile=tile, n_tiles=n_tiles, n_steps=n_steps,
                             n_seq_blocks=n_seq_blocks, layer=layer)

    def cur(s):
        t = jnp.minimum(s, n_steps - 1)
        return t // n_tiles, t % n_tiles

    def prev(s):
        t = jnp.maximum(s - 1, 0)
        return t // n_tiles, t % n_tiles

    return pl.pallas_call(
        kern,
        grid=(n_steps + 1,),
        in_specs=[
            pl.BlockSpec((1, tile, d), lambda s: (*cur(s), 0)),
            pl.BlockSpec((1, BLOCK, d), lambda s: (
                cur(s)[0], jnp.minimum((cur(s)[1] + 1) * n_blk, n_seq_blocks - 1), 0)),
            pl.BlockSpec((1, tile, d), lambda s: (*prev(s), 0)),
            pl.BlockSpec((1, 1, N_MOD, d), lambda s: (layer, cur(s)[0], 0, 0)),
            pl.BlockSpec((1, 1, N_MOD, d), lambda s: (layer, prev(s)[0], 0, 0)),
            _layer_spec(g1.shape, layer),
            _layer_spec(w_in.shape, layer),
            _layer_spec(vg.shape, layer),
            _layer_spec(ws.shape, layer),
            _layer_spec(bs_full.shape, layer),
            _layer_spec(qg.shape, layer),
            _layer_spec(kg.shape, layer),
            pl.BlockSpec(memory_space=pltpu.SMEM),
            _const_spec(bias_tbl.shape),
            _const_spec(seg.shape),
            _layer_spec(w_a.shape, layer),
            _layer_spec(w_b.shape, layer),
            _layer_spec(w_o.shape, layer),
        ],
        out_specs=pl.BlockSpec((1, tile, d), lambda s: (*prev(s), 0)),
        out_shape=jax.ShapeDtypeStruct(x.shape, x.dtype),
        scratch_shapes=[
            pltpu.VMEM((tile + 2 * BLOCK, d), BF16),
            pltpu.VMEM((tile, d), BF16),
            pltpu.VMEM((N_KV_HEADS, n_blk, 2 * BLOCK, PAIR), BF16),
            pltpu.VMEM((N_KV_HEADS, 2, tile + 2 * BLOCK, PAIR), BF16),
            pltpu.VMEM((N_KV_HEADS, 2, tile + 2 * BLOCK, 2 * PAIR), BF16),
            pltpu.VMEM((N_KV_HEADS, 2 * BLOCK, 6 * BLOCK), BF16),
            pltpu.VMEM((tile, GM_WIDTH), BF16),
            pltpu.VMEM((tile, Q_WIDTH), BF16),
            pltpu.VMEM((tile, 2 * D_MODEL), F32),
        ],
        compiler_params=pltpu.CompilerParams(
            dimension_semantics=("arbitrary",), vmem_limit_bytes=VMEM_LIMIT_BYTES),
        name="token_mixer",
    )(x, x, x, mod, mod, g1, w_in, vg, ws, bs_full, qg, kg, sink, bias_tbl, seg, w_a, w_b, w_o)


def _ffn_kernel(x_ref, mod_ref, g2_ref, w1_ref, w2_ref, o_ref):
    sh2 = mod_ref[0, 0, 3:4, :]
    sc2 = mod_ref[0, 0, 4:5, :]
    gt2 = mod_ref[0, 0, 5:6, :]
    x = x_ref[0]
    hb = (_rms(x, g2_ref[0]) * (1.0 + sc2) + sh2).astype(BF16)
    acc = jnp.zeros(x.shape, F32)
    for s in range(0, D_FF, FFN_CHUNK):
        f_gate = _dot(hb, w1_ref[0, :, s:s + FFN_CHUNK])
        f_up = _dot(hb, w1_ref[0, :, D_FF + s:D_FF + s + FFN_CHUNK])
        act = (f_gate * jax.nn.sigmoid(f_gate) * f_up).astype(BF16)
        acc = acc + _dot(act, w2_ref[0, s:s + FFN_CHUNK, :])
    o_ref[0] = x + gt2 * acc


def _ffn(layer, x, mod, g2, w1, w2):
    batch, seq, d = x.shape
    tile = FFN_TILE
    return pl.pallas_call(
        _ffn_kernel,
        grid=(batch, seq // tile),
        in_specs=[
            pl.BlockSpec((1, tile, d), lambda b, i: (b, i, 0)),
            pl.BlockSpec((1, 1, N_MOD, d), lambda b, i: (layer, b, 0, 0)),
            _layer_spec(g2.shape, layer),
            _layer_spec(w1.shape, layer),
            _layer_spec(w2.shape, layer),
        ],
        out_specs=pl.BlockSpec((1, tile, d), lambda b, i: (b, i, 0)),
        out_shape=jax.ShapeDtypeStruct(x.shape, x.dtype),
        compiler_params=pltpu.CompilerParams(
            dimension_semantics=("arbitrary", "arbitrary"), vmem_limit_bytes=VMEM_LIMIT_BYTES),
        name="swiglu_ffn",
    )(x, mod, g2, w1, w2)


def kernel(x, c, w_ada, b_ada, norm1_g, w_in, gm_v_g, gm_w_s, gm_b_s, q_norm_g, k_norm_g, attn_sink,
           w_a, w_b, w_o, norm2_g, w_ffn_in, w_ffn_out):
    n_layers = w_ada.shape[0]
    batch = x.shape[0]
    assert x.shape[1] % MIX_TILE == 0 and x.shape[1] % FFN_TILE == 0 and D_FF % FFN_CHUNK == 0

    mod = _ada_mod(c, w_ada, b_ada).reshape(n_layers, batch, N_MOD, D_MODEL)
    bias_tbl = jnp.asarray(_attn_bias_tables())
    seg = jnp.asarray(_segment_ones(SEG_WIDTH), dtype=BF16)
    bs_full = jnp.repeat(jnp.swapaxes(gm_b_s, 1, 2), GM_GROUP_DIM, axis=2)
    qg = jnp.tile(q_norm_g, (1, N_HEADS))[:, None, :]
    kg = jnp.tile(k_norm_g, (1, N_KV_HEADS))[:, None, :]
    g1, g2, vg = norm1_g[:, None, :], norm2_g[:, None, :], gm_v_g[:, None, :]
    w_in, ws, w_a, w_b, w_o, w1, w2 = (
        w.astype(BF16) for w in (w_in, gm_w_s, w_a, w_b, w_o, w_ffn_in, w_ffn_out))

    for l in range(n_layers):
        x = _mixer(l, x, mod, g1, w_in, vg, ws, bs_full, qg, kg, attn_sink, bias_tbl, seg, w_a, w_b, w_o)
        x = _ffn(l, x, mod, g2, w1, w2)
    return x
```

```python
import functools
import math

import numpy as np
import jax
import jax.numpy as jnp
from jax import lax
from jax.experimental import pallas as pl
from jax.experimental.pallas import tpu as pltpu

D_MODEL = 1024
CHUNK = 128
GM_GROUPS = 8
GM_WIDTH = 512
GM_GROUP_DIM = GM_WIDTH // GM_GROUPS
N_HEADS = 8
N_KV_HEADS = 2
HEAD_DIM = 64
Q_PER_KV = N_HEADS // N_KV_HEADS
Q_WIDTH = N_HEADS * HEAD_DIM
KV_WIDTH = N_KV_HEADS * HEAD_DIM
BLOCK = 128
D_FF = ((-(-8 * D_MODEL // 3) + 255) // 256) * 256
N_MOD = 6
EPS = 1e-6
NEG_INF = -1e30
LOG2E = math.log2(math.e)

COL_U = 0
COL_Q = 2 * GM_WIDTH
COL_K = COL_Q + Q_WIDTH
COL_V = COL_K + KV_WIDTH
COL_GA = COL_V + KV_WIDTH
COL_GB = COL_GA + D_MODEL
IN_WIDTH = COL_GB + D_MODEL

PAIR = 2 * HEAD_DIM
SEG_WIDTH = 256
MIX_TILE = 512
FFN_TILE = 1024
FFN_CHUNK = 256
ADA_TILE = 1536
VMEM_LIMIT_BYTES = 56 * 1024 * 1024

BF16 = jnp.bfloat16
BF16_SUBLANES = 16
F32 = jnp.float32


def _rms(x, gain):
    ms = jnp.mean(x * x, axis=-1, keepdims=True)
    return x * lax.rsqrt(ms + EPS) * gain


def _dot(a, b):
    return jnp.dot(a, b, preferred_element_type=F32)


def _const_spec(shape):
    zeros = (0,) * len(shape)
    return pl.BlockSpec(shape, lambda *_: zeros, pipeline_mode=pl.Buffered(1))


def _convert_rows(n_rows, n_steps):
    rows = BF16_SUBLANES
    while n_rows % rows or n_rows // rows > n_steps:
        rows += BF16_SUBLANES
    return rows


def _layer_spec(shape, layer):
    index = (layer,) + (0,) * (len(shape) - 1)
    return pl.BlockSpec((1,) + tuple(shape[1:]), lambda *_: index, pipeline_mode=pl.Buffered(1))


def _ada_kernel(c_ref, w_ref, b_ref, o_ref):
    c = c_ref[...]
    a = (c * jax.nn.sigmoid(c)).astype(BF16)
    o_ref[0] = _dot(a, w_ref[0].astype(BF16)) + b_ref[0]


def _ada_mod(c, w_ada, b_ada):
    n_layers, d, width = w_ada.shape
    batch = c.shape[0]
    return pl.pallas_call(
        _ada_kernel,
        grid=(n_layers, width // ADA_TILE),
        in_specs=[
            pl.BlockSpec((batch, d), lambda l, j: (0, 0)),
            pl.BlockSpec((1, d, ADA_TILE), lambda l, j: (l, 0, j)),
            pl.BlockSpec((1, 1, ADA_TILE), lambda l, j: (l, 0, j)),
        ],
        out_specs=pl.BlockSpec((1, batch, ADA_TILE), lambda l, j: (l, 0, j)),
        out_shape=jax.ShapeDtypeStruct((n_layers, batch, width), F32),
        compiler_params=pltpu.CompilerParams(
            dimension_semantics=("arbitrary", "arbitrary"), vmem_limit_bytes=VMEM_LIMIT_BYTES),
        name="adaln_mod",
    )(c, w_ada, b_ada.reshape(n_layers, 1, width))


def _attn_bias_tables():
    qi = np.arange(BLOCK)[:, None]
    kj = np.arange(3 * BLOCK)[None, :]
    dist = np.abs(kj - BLOCK - qi)
    slopes = np.exp2(-8.0 * np.arange(1, N_HEADS + 1) / N_HEADS).astype(np.float32)
    bias = -slopes[:, None, None] * dist[None].astype(np.float32) * np.float32(LOG2E)
    tables = []
    for lo, hi in ((BLOCK, 3 * BLOCK), (0, 3 * BLOCK), (0, 2 * BLOCK)):
        valid = (dist <= BLOCK) & (kj >= lo) & (kj < hi)
        tables.append(np.where(valid[None], bias, np.float32(NEG_INF)))
    t = np.stack(tables).astype(np.float32)
    t = t.reshape(3, N_KV_HEADS, 2, 2, BLOCK, 3 * BLOCK)
    return t.transpose(0, 1, 2, 4, 3, 5).reshape(3, N_KV_HEADS, 2 * BLOCK, 6 * BLOCK)


def _segment_ones(width):
    idx = np.arange(width) // HEAD_DIM
    return (idx[:, None] == idx[None, :]).astype(np.float32)


def _head_sumsq(x, seg):
    sq = x * x
    hi = sq.astype(BF16)
    lo = (sq - hi.astype(F32)).astype(BF16)
    return _dot(hi, seg) + _dot(lo, seg)


def _mixer_kernel(x_ref, xn_ref, xprev_ref, mod_ref, modprev_ref, g1_ref, win_ref, vg_ref, ws_ref, bs_ref,
                  qg_ref, kg_ref, sink_ref, bias_ref, seg_ref, wa_ref, wb_ref, wo_ref, w1f_ref, w2f_ref,
                  o_ref, w1b_ref, w2b_ref, hb_ext, merged_buf, q_buf, k_ext, v_ext, e_buf, a_buf, att_buf, gate_buf,
                  *, tile, n_tiles, n_steps, n_seq_blocks, layer):
    n_blk = tile // BLOCK
    n_units = N_KV_HEADS * n_blk
    gate_chunk = 2 * D_MODEL // n_units
    assert gate_chunk * n_blk == D_MODEL
    step = pl.program_id(0)
    w1b_ref[...] = w1f_ref[0].astype(BF16)
    w2b_ref[...] = w2f_ref[0].astype(BF16)
    i = lax.rem(jnp.minimum(step, n_steps - 1), n_tiles)
    sh1 = mod_ref[0, 0, 0:1, :]
    g1s = g1_ref[0] * (1.0 + mod_ref[0, 0, 1:2, :])

    def modnorm(xv):
        ms = jnp.mean(xv * xv, axis=-1, keepdims=True)
        return (xv * lax.rsqrt(ms + EPS) * g1s + sh1).astype(BF16)

    @pl.when(step == 0)
    def _():
        merged_buf[...] = jnp.zeros(merged_buf.shape, BF16)

    @pl.when(i == 0)
    def _():
        hb_ext[0:BLOCK, :] = jnp.zeros((BLOCK, D_MODEL), BF16)
        hb_ext[BLOCK:2 * BLOCK, :] = modnorm(x_ref[0, 0:BLOCK, :])
        k_ext[:, :, 0:BLOCK, :] = jnp.zeros((N_KV_HEADS, 2, BLOCK, PAIR), BF16)
        v_ext[:, :, 0:BLOCK, :] = jnp.zeros((N_KV_HEADS, 2, BLOCK, 2 * PAIR), BF16)

    @pl.when(i != 0)
    def _():
        hb_ext[0:2 * BLOCK, :] = hb_ext[tile:tile + 2 * BLOCK, :]
        k_ext[:, :, 0:BLOCK, :] = k_ext[:, :, tile:tile + BLOCK, :]
        v_ext[:, :, 0:BLOCK, :] = v_ext[:, :, tile:tile + BLOCK, :]

    y_prev = _dot(merged_buf[...], wo_ref[0])
    x = x_ref[0]
    hb_ext[2 * BLOCK:BLOCK + tile, :] = modnorm(x[BLOCK:, :])
    hb_ext[BLOCK + tile:, :] = modnorm(xn_ref[0])
    o_ref[0] = xprev_ref[0] + modprev_ref[0, 0, 2:3, :] * y_prev
    hb = hb_ext[BLOCK:BLOCK + tile, :]

    qkv = _dot(hb_ext[BLOCK:, :], win_ref[:, COL_Q:COL_GA])
    uv = _dot(hb, win_ref[:, COL_U:COL_Q])
    k = qkv[:, Q_WIDTH:Q_WIDTH + KV_WIDTH]
    k = k * lax.rsqrt(_head_sumsq(k, seg_ref[0:KV_WIDTH, 0:KV_WIDTH]) * (1.0 / HEAD_DIM) + EPS) * kg_ref[0]
    v = qkv[:, Q_WIDTH + KV_WIDTH:]
    low = lax.broadcasted_iota(jnp.int32, k.shape, 1) < HEAD_DIM
    for src, dst in ((k, k_ext), (v, v_ext)):
        swapped = pltpu.roll(src, HEAD_DIM, axis=1)
        for j in range(N_KV_HEADS):
            own, other = (src, swapped) if j == 0 else (swapped, src)
            dst[j, 0, BLOCK:, 0:PAIR] = jnp.where(low, own, 0.0).astype(BF16)
            dst[j, 1, BLOCK:, 0:PAIR] = jnp.where(low, 0.0, other).astype(BF16)
    for j in range(N_KV_HEADS):
        v_ext[j, 0, BLOCK:, PAIR:2 * PAIR] = jnp.where(low, 1.0, 0.0).astype(BF16)
        v_ext[j, 1, BLOCK:, PAIR:2 * PAIR] = jnp.where(low, 0.0, 1.0).astype(BF16)

    zq = qkv[0:tile, 0:Q_WIDTH]
    qg = qg_ref[0] * (HEAD_DIM ** -0.5 * LOG2E)
    seg = seg_ref[...]
    pairs_per_seg = SEG_WIDTH // PAIR
    for half in range(Q_WIDTH // SEG_WIDTH):
        cols = slice(half * SEG_WIDTH, (half + 1) * SEG_WIDTH)
        zh = zq[:, cols]
        qn = (zh * lax.rsqrt(_head_sumsq(zh, seg) * (1.0 / HEAD_DIM) + EPS) * qg[:, cols]).astype(BF16)
        for pp in range(pairs_per_seg):
            j, p = divmod(half * pairs_per_seg + pp, 2)
            for n in range(n_blk):
                q_buf[j, n, p * BLOCK:(p + 1) * BLOCK, :] = (
                    qn[n * BLOCK:(n + 1) * BLOCK, pp * PAIR:(pp + 1) * PAIR])

    vg = vg_ref[0]
    low_lane =lax.broadcasted_iota(jnp.int32, (BLOCK, PAIR), 1) < HEAD_DIM

    def unit_logits(n, j):
        band = slice(n * BLOCK, (n + 3) * BLOCK)
        gb = i * n_blk + n
        tix = jnp.where(gb == 0, 0, jnp.where(gb == n_seq_blocks - 1, 2, 1))
        q = q_buf[j, n]
        k = jnp.concatenate([k_ext[j, 0, band, :], k_ext[j, 1, band, :]], axis=0)
        s = lax.dot_general(q, k, (((1,), (1,)), ((), ())), preferred_element_type=F32)
        return s + bias_ref[tix, j]

    def unit_softmax_pv(n, j, s):
        band = slice(n * BLOCK, (n + 3) * BLOCK)
        sink_terms = []
        for p in range(2):
            for par in range(2):
                sg = s[p * BLOCK:(p + 1) * BLOCK, par * 3 * BLOCK:(par + 1) * 3 * BLOCK]
                sink = sink_ref[layer, j * Q_PER_KV + 2 * p + par] * LOG2E
                m = jnp.maximum(jnp.max(sg, axis=-1, keepdims=True), sink)
                e_buf[j, p * BLOCK:(p + 1) * BLOCK, par * 3 * BLOCK:(par + 1) * 3 * BLOCK] = (
                    jnp.exp2(sg - m).astype(BF16))
                sink_terms.append(jnp.exp2(sink - m))
        v = jnp.concatenate([v_ext[j, 0, band, :], v_ext[j, 1, band, :]], axis=0)
        o = _dot(e_buf[j], v)
        for p in range(2):
            rows = slice(p * BLOCK, (p + 1) * BLOCK)
            den = o[rows, PAIR:] + jnp.where(low_lane, sink_terms[2 * p], sink_terms[2 * p + 1])
            att_buf[n * BLOCK:(n + 1) * BLOCK, (2 * j + p) * PAIR:(2 * j + p + 1) * PAIR] = (
                o[rows, :PAIR] / den).astype(BF16)

    def gmlp_chunk(c):
        rows = slice(c * CHUNK, (c + 1) * CHUNK)
        u = jax.nn.gelu(uv[rows, :GM_WIDTH])
        v = _rms(jax.nn.gelu(uv[rows, GM_WIDTH:]), vg).astype(BF16)
        mixed = jnp.concatenate(
            [_dot(ws_ref[0, g], v[:, g * GM_GROUP_DIM:(g + 1) * GM_GROUP_DIM]) for g in range(GM_GROUPS)],
            axis=1)
        a_buf[rows, :] = (u * (mixed + bs_ref[0])).astype(BF16)

    s_next = unit_logits(0, 0)
    for unit in range(n_units):
        n, j = divmod(unit, N_KV_HEADS)
        s_cur = s_next
        if unit + 1 < n_units:
            s_next = unit_logits(*divmod(unit + 1, N_KV_HEADS))
        if unit < n_blk:
            gmlp_chunk(unit)
        else:
            c1 = (unit - n_blk) * gate_chunk
            gate_buf[:, c1:c1 + gate_chunk] = jax.nn.sigmoid(gate_buf[:, c1:c1 + gate_chunk])
        unit_softmax_pv(n, j, s_cur)
        if unit % 2 == 0:
            c0, c2 = unit * gate_chunk, (unit + 2) * gate_chunk
            gate_buf[:, c0:c2] = _dot(hb, win_ref[:, COL_GA + c0:COL_GA + c2])

    merged = gate_buf[:, :D_MODEL] * _dot(a_buf[...], wa_ref[0])
    merged = merged + jax.nn.sigmoid(gate_buf[:, D_MODEL:]) * _dot(att_buf[...], wb_ref[0])
    merged_buf[...] = merged.astype(BF16)


def _mixer(layer, x, mod, g1, w_in, vg, ws, bs_full, qg, kg, sink, bias_tbl, seg, w_a, w_b, w_o, w1_f32, w2_f32):
    batch, seq, d = x.shape
    tile = MIX_TILE
    n_blk = tile // BLOCK
    n_seq_blocks = seq // BLOCK
    n_tiles = seq // tile
    n_steps = batch * n_tiles
    kern = functools.partial(_mixer_kernel, tile=tile, n_tiles=n_tiles, n_steps=n_steps,
                             n_seq_blocks=n_seq_blocks, layer=layer)

    def cur(s):
        t = jnp.minimum(s, n_steps - 1)
        return t // n_tiles, t % n_tiles

    w1_rows, w2_rows = _convert_rows(w1_f32.shape[1], n_steps), _convert_rows(w2_f32.shape[1], n_steps)
    w1_blocks, w2_blocks = w1_f32.shape[1] // w1_rows, w2_f32.shape[1] // w2_rows

    def prev(s):
        t = jnp.maximum(s - 1, 0)
        return t // n_tiles, t % n_tiles

    return pl.pallas_call(
        kern,
        grid=(n_steps + 1,),
        in_specs=[
            pl.BlockSpec((1, tile, d), lambda s: (*cur(s), 0)),
            pl.BlockSpec((1, BLOCK, d), lambda s: (
                cur(s)[0], jnp.minimum((cur(s)[1] + 1) * n_blk, n_seq_blocks - 1), 0)),
            pl.BlockSpec((1, tile, d), lambda s: (*prev(s), 0)),
            pl.BlockSpec((1, 1, N_MOD, d), lambda s: (layer, cur(s)[0], 0, 0)),
            pl.BlockSpec((1, 1, N_MOD, d), lambda s: (layer, prev(s)[0], 0, 0)),
            _layer_spec(g1.shape, layer),
            _const_spec(w_in.shape),
            _layer_spec(vg.shape, layer),
            _layer_spec(ws.shape, layer),
            _layer_spec(bs_full.shape, layer),
            _layer_spec(qg.shape, layer),
            _layer_spec(kg.shape, layer),
            pl.BlockSpec(memory_space=pltpu.SMEM),
            _const_spec(bias_tbl.shape),
            _const_spec(seg.shape),
            _layer_spec(w_a.shape, layer),
            _layer_spec(w_b.shape, layer),
            _layer_spec(w_o.shape, layer),
            pl.BlockSpec((1, w1_rows) + w1_f32.shape[2:], lambda s: (layer, jnp.minimum(s, w1_blocks - 1), 0)),
            pl.BlockSpec((1, w2_rows) + w2_f32.shape[2:], lambda s: (layer, jnp.minimum(s, w2_blocks - 1), 0)),
        ],
        out_specs=[
            pl.BlockSpec((1, tile, d), lambda s: (*prev(s), 0)),
            pl.BlockSpec((w1_rows,) + w1_f32.shape[2:], lambda s: (jnp.minimum(s, w1_blocks - 1), 0)),
            pl.BlockSpec((w2_rows,) + w2_f32.shape[2:], lambda s: (jnp.minimum(s, w2_blocks - 1), 0)),
        ],
        out_shape=[
            jax.ShapeDtypeStruct(x.shape, x.dtype),
            jax.ShapeDtypeStruct(w1_f32.shape[1:], BF16),
            jax.ShapeDtypeStruct(w2_f32.shape[1:], BF16),
        ],
        scratch_shapes=[
            pltpu.VMEM((tile + 2 * BLOCK, d), BF16),
            pltpu.VMEM((tile, d), BF16),
            pltpu.VMEM((N_KV_HEADS, n_blk, 2 * BLOCK, PAIR), BF16),
            pltpu.VMEM((N_KV_HEADS, 2, tile + 2 * BLOCK, PAIR), BF16),
            pltpu.VMEM((N_KV_HEADS, 2, tile + 2 * BLOCK, 2 * PAIR), BF16),
            pltpu.VMEM((N_KV_HEADS, 2 * BLOCK, 6 * BLOCK), BF16),
            pltpu.VMEM((tile, GM_WIDTH), BF16),
            pltpu.VMEM((tile, Q_WIDTH), BF16),
            pltpu.VMEM((tile, 2 * D_MODEL), F32),
        ],
        compiler_params=pltpu.CompilerParams(
            dimension_semantics=("arbitrary",), vmem_limit_bytes=VMEM_LIMIT_BYTES),
        name="token_mixer",
    )(x, x, x, mod, mod, g1, w_in, vg, ws, bs_full, qg, kg, sink, bias_tbl, seg, w_a, w_b, w_o, w1_f32, w2_f32)


def _ffn_kernel(x_ref, mod_ref, g2_ref, w1_ref, w2_ref, *rest):
    o_ref = rest[-1] if len(rest) == 1 else rest[1]
    if len(rest) == 3:
        rest[2][...] = rest[0][0].astype(BF16)
    sh2 = mod_ref[0, 0, 3:4, :]
    sc2 = mod_ref[0, 0, 4:5, :]
    gt2 = mod_ref[0, 0, 5:6, :]
    x = x_ref[0]
    hb = (_rms(x, g2_ref[0]) * (1.0 + sc2) + sh2).astype(BF16)
    acc = jnp.zeros(x.shape, F32)
    for s in range(0, D_FF, FFN_CHUNK):
        f_gate = _dot(hb, w1_ref[:, s:s + FFN_CHUNK])
        f_up = _dot(hb, w1_ref[:, D_FF + s:D_FF + s + FFN_CHUNK])
        act = (f_gate * jax.nn.sigmoid(f_gate) * f_up).astype(BF16)
        acc = acc + _dot(act, w2_ref[s:s + FFN_CHUNK, :])
    o_ref[0] = x + gt2 * acc


def _ffn(layer, x, mod, g2, w1, w2, w_in_f32):
    batch, seq, d = x.shape
    tile = FFN_TILE
    n_tiles = seq // tile
    in_specs = [
        pl.BlockSpec((1, tile, d), lambda b, i: (b, i, 0)),
        pl.BlockSpec((1, 1, N_MOD, d), lambda b, i: (layer, b, 0, 0)),
        _layer_spec(g2.shape, layer),
        _const_spec(w1.shape),
        _const_spec(w2.shape),
    ]
    out_specs = [pl.BlockSpec((1, tile, d), lambda b, i: (b, i, 0))]
    out_shape = [jax.ShapeDtypeStruct(x.shape, x.dtype)]
    args = [x, mod, g2, w1, w2]
    convert_next = layer + 1 < w_in_f32.shape[0]
    if convert_next:
        rows = _convert_rows(w_in_f32.shape[1], batch * n_tiles)
        n_blocks = w_in_f32.shape[1] // rows

        def blk(b, i):
            return jnp.minimum(b * n_tiles + i, n_blocks - 1)

        in_specs.append(pl.BlockSpec((1, rows) + w_in_f32.shape[2:], lambda b, i: (layer + 1, blk(b, i), 0)))
        out_specs.append(pl.BlockSpec((rows,) + w_in_f32.shape[2:], lambda b, i: (blk(b, i), 0)))
        out_shape.append(jax.ShapeDtypeStruct(w_in_f32.shape[1:], BF16))
        args.append(w_in_f32)
    outs = pl.pallas_call(
        _ffn_kernel,
        grid=(batch, n_tiles),
        in_specs=in_specs,
        out_specs=out_specs,
        out_shape=out_shape,
        compiler_params=pltpu.CompilerParams(
            dimension_semantics=("arbitrary", "arbitrary"), vmem_limit_bytes=VMEM_LIMIT_BYTES),
        name="swiglu_ffn",
    )(*args)
    return (outs[0], outs[1]) if convert_next else (outs[0], None)


def kernel(x, c, w_ada, b_ada, norm1_g, w_in, gm_v_g, gm_w_s, gm_b_s, q_norm_g, k_norm_g, attn_sink,
           w_a, w_b, w_o, norm2_g, w_ffn_in, w_ffn_out):
    n_layers = w_ada.shape[0]
    batch = x.shape[0]
    assert x.shape[1] % MIX_TILE == 0 and x.shape[1] % FFN_TILE == 0 and D_FF % FFN_CHUNK == 0

    mod = _ada_mod(c, w_ada, b_ada).reshape(n_layers, batch, N_MOD, D_MODEL)
    bias_tbl = jnp.asarray(_attn_bias_tables())
    seg = jnp.asarray(_segment_ones(SEG_WIDTH), dtype=BF16)
    bs_full = jnp.repeat(jnp.swapaxes(gm_b_s, 1, 2), GM_GROUP_DIM, axis=2)
    qg = jnp.tile(q_norm_g, (1, N_HEADS))[:, None, :]
    kg = jnp.tile(k_norm_g, (1, N_KV_HEADS))[:, None, :]
    g1, g2, vg = norm1_g[:, None, :], norm2_g[:, None, :], gm_v_g[:, None, :]
    ws, w_a, w_b, w_o = (w.astype(BF16) for w in (gm_w_s, w_a, w_b, w_o))
    w_in_l = w_in[0].astype(BF16)

    for l in range(n_layers):
        x, w1, w2 = _mixer(l, x, mod, g1, w_in_l, vg, ws, bs_full, qg, kg, attn_sink, bias_tbl, seg,
                           w_a, w_b, w_o, w_ffn_in, w_ffn_out)
        x, w_in_l = _ffn(l, x, mod, g2, w1, w2, w_in)
    return x
```

```python
import functools
import math

import numpy as np
import jax
import jax.numpy as jnp
from jax import lax
from jax.experimental import pallas as pl
from jax.experimental.pallas import tpu as pltpu

D_MODEL = 1024
CHUNK = 128
GM_GROUPS = 8
GM_WIDTH = 512
GM_GROUP_DIM = GM_WIDTH // GM_GROUPS
N_HEADS = 8
N_KV_HEADS = 2
HEAD_DIM = 64
Q_PER_KV = N_HEADS // N_KV_HEADS
Q_WIDTH = N_HEADS * HEAD_DIM
KV_WIDTH = N_KV_HEADS * HEAD_DIM
BLOCK = 128
D_FF = ((-(-8 * D_MODEL // 3) + 255) // 256) * 256
N_MOD = 6
EPS = 1e-6
NEG_INF = -1e30
LOG2E = math.log2(math.e)

COL_U = 0
COL_Q = 2 * GM_WIDTH
COL_K = COL_Q + Q_WIDTH
COL_V = COL_K + KV_WIDTH
COL_GA = COL_V + KV_WIDTH
COL_GB = COL_GA + D_MODEL
IN_WIDTH = COL_GB + D_MODEL

PAIR = 2 * HEAD_DIM
SEG_WIDTH = 256
MIX_TILE = 512
FFN_TILE = 1024
FFN_CHUNK = 256
ADA_TILE = 1536
VMEM_LIMIT_BYTES = 56 * 1024 * 1024

BF16 = jnp.bfloat16
BF16_SUBLANES = 16
F32 = jnp.float32


def _rms(x, gain):
    ms = jnp.mean(x * x, axis=-1, keepdims=True)
    return x * lax.rsqrt(ms + EPS) * gain


def _dot(a, b):
    return jnp.dot(a, b, preferred_element_type=F32)


def _const_spec(shape):
    zeros = (0,) * len(shape)
    return pl.BlockSpec(shape, lambda *_: zeros, pipeline_mode=pl.Buffered(1))


def _convert_rows(n_rows, n_steps):
    rows = BF16_SUBLANES
    while n_rows % rows or n_rows // rows > n_steps:
        rows += BF16_SUBLANES
    return rows


def _layer_spec(shape, layer):
    index = (layer,) + (0,) * (len(shape) - 1)
    return pl.BlockSpec((1,) + tuple(shape[1:]), lambda *_: index, pipeline_mode=pl.Buffered(1))


def _ada_kernel(c_ref, w_ref, b_ref, o_ref):
    c = c_ref[...]
    a = (c * jax.nn.sigmoid(c)).astype(BF16)
    o_ref[0] = _dot(a, w_ref[0].astype(BF16)) + b_ref[0]


def _ada_mod(c, w_ada, b_ada):
    n_layers, d, width = w_ada.shape
    batch = c.shape[0]
    return pl.pallas_call(
        _ada_kernel,
        grid=(n_layers, width // ADA_TILE),
        in_specs=[
            pl.BlockSpec((batch, d), lambda l, j: (0, 0)),
            pl.BlockSpec((1, d, ADA_TILE), lambda l, j: (l, 0, j)),
            pl.BlockSpec((1, 1, ADA_TILE), lambda l, j: (l, 0, j)),
        ],
        out_specs=pl.BlockSpec((1, batch, ADA_TILE), lambda l, j: (l, 0, j)),
        out_shape=jax.ShapeDtypeStruct((n_layers, batch, width), F32),
        compiler_params=pltpu.CompilerParams(
            dimension_semantics=("arbitrary", "arbitrary"), vmem_limit_bytes=VMEM_LIMIT_BYTES),
        name="adaln_mod",
    )(c, w_ada, b_ada.reshape(n_layers, 1, width))


def _attn_bias_tables():
    qi = np.arange(BLOCK)[:, None]
    kj = np.arange(3 * BLOCK)[None, :]
    dist = np.abs(kj - BLOCK - qi)
    slopes = np.exp2(-8.0 * np.arange(1, N_HEADS + 1) / N_HEADS).astype(np.float32)
    bias = -slopes[:, None, None] * dist[None].astype(np.float32) * np.float32(LOG2E)
    tables = []
    for lo, hi in ((BLOCK, 3 * BLOCK), (0, 3 * BLOCK), (0, 2 * BLOCK)):
        valid = (dist <= BLOCK) & (kj >= lo) & (kj < hi)
        tables.append(np.where(valid[None], bias, np.float32(NEG_INF)))
    t = np.stack(tables).astype(np.float32)
    t = t.reshape(3, N_KV_HEADS, 2, 2, BLOCK, 3 * BLOCK)
    return t.transpose(0, 1, 2, 4, 3, 5).reshape(3, N_KV_HEADS, 2 * BLOCK, 6 * BLOCK)


def _segment_ones(width):
    idx = np.arange(width) // HEAD_DIM
    return (idx[:, None] == idx[None, :]).astype(np.float32)


def _head_sumsq(x, seg):
    sq = x * x
    hi = sq.astype(BF16)
    lo = (sq - hi.astype(F32)).astype(BF16)
    return _dot(hi, seg) + _dot(lo, seg)


def _mixer_kernel(x_ref, xn_ref, xprev_ref, mod_ref, modprev_ref, g1_ref, win_ref, vg_ref, ws_ref, bs_ref,
                  qg_ref, kg_ref, sink_ref, bias_ref, seg_ref, wa_ref, wb_ref, wo_ref, w1f_ref, w2f_ref,
                  o_ref, w1b_ref, w2b_ref, hb_ext, merged_buf, q_buf, k_ext, v_ext, e_buf, a_buf, att_buf, gate_buf,
                  *, tile, n_tiles, n_steps, n_seq_blocks, layer):
    n_blk = tile // BLOCK
    n_units = N_KV_HEADS * n_blk
    gate_chunk = 2 * D_MODEL // n_units
    assert gate_chunk * n_blk == D_MODEL
    step = pl.program_id(0)
    w1b_ref[...] = w1f_ref[0].astype(BF16)
    w2b_ref[...] = w2f_ref[0].astype(BF16)
    i = lax.rem(jnp.minimum(step, n_steps - 1), n_tiles)
    sh1 = mod_ref[0, 0, 0:1, :]
    g1s = g1_ref[0] * (1.0 + mod_ref[0, 0, 1:2, :])

    def modnorm(xv):
        ms = jnp.mean(xv * xv, axis=-1, keepdims=True)
        return (xv * lax.rsqrt(ms + EPS) * g1s + sh1).astype(BF16)

    @pl.when(step == 0)
    def _():
        merged_buf[...] = jnp.zeros(merged_buf.shape, BF16)

    @pl.when(i == 0)
    def _():
        hb_ext[0:BLOCK, :] = jnp.zeros((BLOCK, D_MODEL), BF16)
        hb_ext[BLOCK:2 * BLOCK, :] = modnorm(x_ref[0, 0:BLOCK, :])
        k_ext[:, :, 0:BLOCK, :] = jnp.zeros((N_KV_HEADS, 2, BLOCK, PAIR), BF16)
        v_ext[:, :, 0:BLOCK, :] = jnp.zeros((N_KV_HEADS, 2, BLOCK, 2 * PAIR), BF16)

    @pl.when(i != 0)
    def _():
        hb_ext[0:2 * BLOCK, :] = hb_ext[tile:tile + 2 * BLOCK, :]
        k_ext[:, :, 0:BLOCK, :] = k_ext[:, :, tile:tile + BLOCK, :]
        v_ext[:, :, 0:BLOCK, :] = v_ext[:, :, tile:tile + BLOCK, :]

    y_prev = _dot(merged_buf[...], wo_ref[0])
    x = x_ref[0]
    hb_ext[2 * BLOCK:BLOCK + tile, :] = modnorm(x[BLOCK:, :])
    hb_ext[BLOCK + tile:, :] = modnorm(xn_ref[0])
    o_ref[0] = xprev_ref[0] + modprev_ref[0, 0, 2:3, :] * y_prev
    hb = hb_ext[BLOCK:BLOCK + tile, :]

    kv = _dot(hb_ext[BLOCK:, :], win_ref[:, COL_K:COL_GA])
    zq = _dot(hb, win_ref[:, COL_Q:COL_K])
    uv = _dot(hb, win_ref[:, COL_U:COL_Q])
    k = kv[:, :KV_WIDTH]
    k = k * lax.rsqrt(_head_sumsq(k, seg_ref[0:KV_WIDTH, 0:KV_WIDTH]) * (1.0 / HEAD_DIM) + EPS) * kg_ref[0]
    v = kv[:, KV_WIDTH:]
    low = lax.broadcasted_iota(jnp.int32, k.shape, 1) < HEAD_DIM
    for src, dst in ((k, k_ext), (v, v_ext)):
        swapped = pltpu.roll(src, HEAD_DIM, axis=1)
        for j in range(N_KV_HEADS):
            own, other = (src, swapped) if j == 0 else (swapped, src)
            dst[j, 0, BLOCK:, 0:PAIR] = jnp.where(low, own, 0.0).astype(BF16)
            dst[j, 1, BLOCK:, 0:PAIR] = jnp.where(low, 0.0, other).astype(BF16)
    for j in range(N_KV_HEADS):
        v_ext[j, 0, BLOCK:, PAIR:2 * PAIR] = jnp.where(low, 1.0, 0.0).astype(BF16)
        v_ext[j, 1, BLOCK:, PAIR:2 * PAIR] = jnp.where(low, 0.0, 1.0).astype(BF16)

    qg = qg_ref[0] * (HEAD_DIM ** -0.5 * LOG2E)
    seg = seg_ref[...]
    pairs_per_seg = SEG_WIDTH // PAIR
    for half in range(Q_WIDTH // SEG_WIDTH):
        cols = slice(half * SEG_WIDTH, (half + 1) * SEG_WIDTH)
        zh = zq[:, cols]
        qn = (zh * lax.rsqrt(_head_sumsq(zh, seg) * (1.0 / HEAD_DIM) + EPS) * qg[:, cols]).astype(BF16)
        for pp in range(pairs_per_seg):
            j, p = divmod(half * pairs_per_seg + pp, 2)
            for n in range(n_blk):
                q_buf[j, n, p * BLOCK:(p + 1) * BLOCK, :] = (
                    qn[n * BLOCK:(n + 1) * BLOCK, pp * PAIR:(pp + 1) * PAIR])

    vg = vg_ref[0]
    low_lane =lax.broadcasted_iota(jnp.int32, (BLOCK, PAIR), 1) < HEAD_DIM

    def unit_logits(n, j):
        band = slice(n * BLOCK, (n + 3) * BLOCK)
        gb = i * n_blk + n
        tix = jnp.where(gb == 0, 0, jnp.where(gb == n_seq_blocks - 1, 2, 1))
        q = q_buf[j, n]
        k = jnp.concatenate([k_ext[j, 0, band, :], k_ext[j, 1, band, :]], axis=0)
        s = lax.dot_general(q, k, (((1,), (1,)), ((), ())), preferred_element_type=F32)
        return s + bias_ref[tix, j]

    def unit_softmax_pv(n, j, s):
        band = slice(n * BLOCK, (n + 3) * BLOCK)
        sink_terms = []
        for p in range(2):
            for par in range(2):
                sg = s[p * BLOCK:(p + 1) * BLOCK, par * 3 * BLOCK:(par + 1) * 3 * BLOCK]
                sink = sink_ref[layer, j * Q_PER_KV + 2 * p + par] * LOG2E
                m = jnp.maximum(jnp.max(sg, axis=-1, keepdims=True), sink)
                e_buf[j, p * BLOCK:(p + 1) * BLOCK, par * 3 * BLOCK:(par + 1) * 3 * BLOCK] = (
                    jnp.exp2(sg - m).astype(BF16))
                sink_terms.append(jnp.exp2(sink - m))
        v = jnp.concatenate([v_ext[j, 0, band, :], v_ext[j, 1, band, :]], axis=0)
        o = _dot(e_buf[j], v)
        for p in range(2):
            rows = slice(p * BLOCK, (p + 1) * BLOCK)
            den = o[rows, PAIR:] + jnp.where(low_lane, sink_terms[2 * p], sink_terms[2 * p + 1])
            att_buf[n * BLOCK:(n + 1) * BLOCK, (2 * j + p) * PAIR:(2 * j + p + 1) * PAIR] = (
                o[rows, :PAIR] / den).astype(BF16)

    def gmlp_chunk(c):
        rows = slice(c * CHUNK, (c + 1) * CHUNK)
        u = jax.nn.gelu(uv[rows, :GM_WIDTH])
        v = _rms(jax.nn.gelu(uv[rows, GM_WIDTH:]), vg).astype(BF16)
        mixed = jnp.concatenate(
            [_dot(ws_ref[0, g], v[:, g * GM_GROUP_DIM:(g + 1) * GM_GROUP_DIM]) for g in range(GM_GROUPS)],
            axis=1)
        a_buf[rows, :] = (u * (mixed + bs_ref[0])).astype(BF16)

    s_next = unit_logits(0, 0)
    for unit in range(n_units):
        n, j = divmod(unit, N_KV_HEADS)
        s_cur = s_next
        if unit + 1 < n_units:
            s_next = unit_logits(*divmod(unit + 1, N_KV_HEADS))
        if unit < n_blk:
            gmlp_chunk(unit)
        else:
            c1 = (unit - n_blk) * gate_chunk
            gate_buf[:, c1:c1 + gate_chunk] = jax.nn.sigmoid(gate_buf[:, c1:c1 + gate_chunk])
        unit_softmax_pv(n, j, s_cur)
        if unit % 2 == 0:
            c0, c2 = unit * gate_chunk, (unit + 2) * gate_chunk
            gate_buf[:, c0:c2] = _dot(hb, win_ref[:, COL_GA + c0:COL_GA + c2])

    merged = gate_buf[:, :D_MODEL] * _dot(a_buf[...], wa_ref[0])
    merged = merged + jax.nn.sigmoid(gate_buf[:, D_MODEL:]) * _dot(att_buf[...], wb_ref[0])
    merged_buf[...] = merged.astype(BF16)


def _mixer(layer, x, mod, g1, w_in, vg, ws, bs_full, qg, kg, sink, bias_tbl, seg, w_a, w_b, w_o, w1_f32, w2_f32):
    batch, seq, d = x.shape
    tile = MIX_TILE
    n_blk = tile // BLOCK
    n_seq_blocks = seq // BLOCK
    n_tiles = seq // tile
    n_steps = batch * n_tiles
    kern = functools.partial(_mixer_kernel, tile=tile, n_tiles=n_tiles, n_steps=n_steps,
                             n_seq_blocks=n_seq_blocks, layer=layer)

    def cur(s):
        t = jnp.minimum(s, n_steps - 1)
        return t // n_tiles, t % n_tiles

    w1_rows, w2_rows = _convert_rows(w1_f32.shape[1], n_steps), _convert_rows(w2_f32.shape[1], n_steps)
    w1_blocks, w2_blocks = w1_f32.shape[1] // w1_rows, w2_f32.shape[1] // w2_rows

    def prev(s):
        t = jnp.maximum(s - 1, 0)
        return t // n_tiles, t % n_tiles

    return pl.pallas_call(
        kern,
        grid=(n_steps + 1,),
        in_specs=[
            pl.BlockSpec((1, tile, d), lambda s: (*cur(s), 0)),
            pl.BlockSpec((1, BLOCK, d), lambda s: (
                cur(s)[0], jnp.minimum((cur(s)[1] + 1) * n_blk, n_seq_blocks - 1), 0)),
            pl.BlockSpec((1, tile, d), lambda s: (*prev(s), 0)),
            pl.BlockSpec((1, 1, N_MOD, d), lambda s: (layer, cur(s)[0], 0, 0)),
            pl.BlockSpec((1, 1, N_MOD, d), lambda s: (layer, prev(s)[0], 0, 0)),
            _layer_spec(g1.shape, layer),
            _const_spec(w_in.shape),
            _layer_spec(vg.shape, layer),
            _layer_spec(ws.shape, layer),
            _layer_spec(bs_full.shape, layer),
            _layer_spec(qg.shape, layer),
            _layer_spec(kg.shape, layer),
            pl.BlockSpec(memory_space=pltpu.SMEM),
            _const_spec(bias_tbl.shape),
            _const_spec(seg.shape),
            _layer_spec(w_a.shape, layer),
            _layer_spec(w_b.shape, layer),
            _layer_spec(w_o.shape, layer),
            pl.BlockSpec((1, w1_rows) + w1_f32.shape[2:], lambda s: (layer, jnp.minimum(s, w1_blocks - 1), 0)),
            pl.BlockSpec((1, w2_rows) + w2_f32.shape[2:], lambda s: (layer, jnp.minimum(s, w2_blocks - 1), 0)),
        ],
        out_specs=[
            pl.BlockSpec((1, tile, d), lambda s: (*prev(s), 0)),
            pl.BlockSpec((w1_rows,) + w1_f32.shape[2:], lambda s: (jnp.minimum(s, w1_blocks - 1), 0)),
            pl.BlockSpec((w2_rows,) + w2_f32.shape[2:], lambda s: (jnp.minimum(s, w2_blocks - 1), 0)),
        ],
        out_shape=[
            jax.ShapeDtypeStruct(x.shape, x.dtype),
            jax.ShapeDtypeStruct(w1_f32.shape[1:], BF16),
            jax.ShapeDtypeStruct(w2_f32.shape[1:], BF16),
        ],
        scratch_shapes=[
            pltpu.VMEM((tile + 2 * BLOCK, d), BF16),
            pltpu.VMEM((tile, d), BF16),
            pltpu.VMEM((N_KV_HEADS, n_blk, 2 * BLOCK, PAIR), BF16),
            pltpu.VMEM((N_KV_HEADS, 2, tile + 2 * BLOCK, PAIR), BF16),
            pltpu.VMEM((N_KV_HEADS, 2, tile + 2 * BLOCK, 2 * PAIR), BF16),
            pltpu.VMEM((N_KV_HEADS, 2 * BLOCK, 6 * BLOCK), BF16),
            pltpu.VMEM((tile, GM_WIDTH), BF16),
            pltpu.VMEM((tile, Q_WIDTH), BF16),
            pltpu.VMEM((tile, 2 * D_MODEL), F32),
        ],
        compiler_params=pltpu.CompilerParams(
            dimension_semantics=("arbitrary",), vmem_limit_bytes=VMEM_LIMIT_BYTES),
        name="token_mixer",
    )(x, x, x, mod, mod, g1, w_in, vg, ws, bs_full, qg, kg, sink, bias_tbl, seg, w_a, w_b, w_o, w1_f32, w2_f32)


def _ffn_kernel(x_ref, mod_ref, g2_ref, w1_ref, w2_ref, *rest):
    o_ref = rest[-1] if len(rest) == 1 else rest[1]
    if len(rest) == 3:
        rest[2][...] = rest[0][0].astype(BF16)
    sh2 = mod_ref[0, 0, 3:4, :]
    sc2 = mod_ref[0, 0, 4:5, :]
    gt2 = mod_ref[0, 0, 5:6, :]
    x = x_ref[0]
    hb = (_rms(x, g2_ref[0]) * (1.0 + sc2) + sh2).astype(BF16)
    acc = jnp.zeros(x.shape, F32)
    for s in range(0, D_FF, FFN_CHUNK):
        f_gate = _dot(hb, w1_ref[:, s:s + FFN_CHUNK])
        f_up = _dot(hb, w1_ref[:, D_FF + s:D_FF + s + FFN_CHUNK])
        act = (f_gate * jax.nn.sigmoid(f_gate) * f_up).astype(BF16)
        acc = acc + _dot(act, w2_ref[s:s + FFN_CHUNK, :])
    o_ref[0] = x + gt2 * acc


def _ffn(layer, x, mod, g2, w1, w2, w_in_f32):
    batch, seq, d = x.shape
    tile = FFN_TILE
    n_tiles = seq // tile
    in_specs = [
        pl.BlockSpec((1, tile, d), lambda b, i: (b, i, 0)),
        pl.BlockSpec((1, 1, N_MOD, d), lambda b, i: (layer, b, 0, 0)),
        _layer_spec(g2.shape, layer),
        _const_spec(w1.shape),
        _const_spec(w2.shape),
    ]
    out_specs = [pl.BlockSpec((1, tile, d), lambda b, i: (b, i, 0))]
    out_shape = [jax.ShapeDtypeStruct(x.shape, x.dtype)]
    args = [x, mod, g2, w1, w2]
    convert_next = layer + 1 < w_in_f32.shape[0]
    if convert_next:
        rows = _convert_rows(w_in_f32.shape[1], batch * n_tiles)
        n_blocks = w_in_f32.shape[1] // rows

        def blk(b, i):
            return jnp.minimum(b * n_tiles + i, n_blocks - 1)

        in_specs.append(pl.BlockSpec((1, rows) + w_in_f32.shape[2:], lambda b, i: (layer + 1, blk(b, i), 0)))
        out_specs.append(pl.BlockSpec((rows,) + w_in_f32.shape[2:], lambda b, i: (blk(b, i), 0)))
        out_shape.append(jax.ShapeDtypeStruct(w_in_f32.shape[1:], BF16))
        args.append(w_in_f32)
    outs = pl.pallas_call(
        _ffn_kernel,
        grid=(batch, n_tiles),
        in_specs=in_specs,
        out_specs=out_specs,
        out_shape=out_shape,
        compiler_params=pltpu.CompilerParams(
            dimension_semantics=("arbitrary", "arbitrary"), vmem_limit_bytes=VMEM_LIMIT_BYTES),
        name="swiglu_ffn",
    )(*args)
    return (outs[0], outs[1]) if convert_next else (outs[0], None)


def kernel(x, c, w_ada, b_ada, norm1_g, w_in, gm_v_g, gm_w_s, gm_b_s, q_norm_g, k_norm_g, attn_sink,
           w_a, w_b, w_o, norm2_g, w_ffn_in, w_ffn_out):
    n_layers = w_ada.shape[0]
    batch = x.shape[0]
    assert x.shape[1] % MIX_TILE == 0 and x.shape[1] % FFN_TILE == 0 and D_FF % FFN_CHUNK == 0

    mod = _ada_mod(c, w_ada, b_ada).reshape(n_layers, batch, N_MOD, D_MODEL)
    bias_tbl = jnp.asarray(_attn_bias_tables())
    seg = jnp.asarray(_segment_ones(SEG_WIDTH), dtype=BF16)
    bs_full = jnp.repeat(jnp.swapaxes(gm_b_s, 1, 2), GM_GROUP_DIM, axis=2)
    qg = jnp.tile(q_norm_g, (1, N_HEADS))[:, None, :]
    kg = jnp.tile(k_norm_g, (1, N_KV_HEADS))[:, None, :]
    g1, g2, vg = norm1_g[:, None, :], norm2_g[:, None, :], gm_v_g[:, None, :]
    ws, w_a, w_b, w_o = (w.astype(BF16) for w in (gm_w_s, w_a, w_b, w_o))
    w_in_l = w_in[0].astype(BF16)

    for l in range(n_layers):
        x, w1, w2 = _mixer(l, x, mod, g1, w_in_l, vg, ws, bs_full, qg, kg, attn_sink, bias_tbl, seg,
                           w_a, w_b, w_o, w_ffn_in, w_ffn_out)
        x, w_in_l = _ffn(l, x, mod, g2, w1, w2, w_in)
    return x
```
